```python
import jax, jax.numpy as jnp
from jax import lax
import numpy as np

D_MODEL = 2048
BATCH = 4
SEQ = 8192
DEPTH = 1
DEC_BATCH = 32
DEC_SEQ = 16
PAST_LEN = 1024

CHUNK = 64
HG_HEADS = 8
HG_DK = 128
HG_DV = 128
HG_WIDTH = HG_HEADS * HG_DK
ML_HEADS = 4
ML_DK = 128
ML_DV = 256
ML_QK_WIDTH = 2 * ML_HEADS * ML_DK
ML_V_WIDTH = ML_HEADS * ML_DV
CONV_W = 4
N_EXPERTS = 256
TOP_K = 8
N_GROUPS = 8
TOPK_GROUPS = 4
D_EXPERT = 512
D_SHARED = 512
ROUTED_SCALE = 2.5
MOE_BLOCK = 64
LN_EPS = 1e-5
DEEPNORM_ALPHA = (2.0 * DEPTH) ** 0.25
DEEPNORM_BETA = (8.0 * DEPTH) ** -0.25
IN_SPLITS = (HG_WIDTH, HG_WIDTH, HG_WIDTH, HG_WIDTH, ML_QK_WIDTH, ML_V_WIDTH, ML_V_WIDTH, ML_HEADS, ML_HEADS, D_MODEL, D_MODEL)
D_IN = 4 * HG_WIDTH + ML_QK_WIDTH + 2 * ML_V_WIDTH + 2 * ML_HEADS + 2 * D_MODEL

kernel_name = 'hybrid_streaming_hgrn2_mlstm_moe'


def layer_norm(x, w=None, b=None):
    xf = x.astype(jnp.float32)
    mu = jnp.mean(xf, axis=-1, keepdims=True)
    var = jnp.mean(jnp.square(xf - mu), axis=-1, keepdims=True)
    y = (xf - mu) * lax.rsqrt(var + LN_EPS)
    if w is not None:
        y = y * w.astype(jnp.float32) + b.astype(jnp.float32)
    return y.astype(x.dtype)


def split_cols(z):
    offsets = []
    acc = 0
    for width in IN_SPLITS[:-1]:
        acc += width
        offsets.append(acc)
    return jnp.split(z, offsets, axis=-1)


def to_heads(a, n_heads):
    return a.reshape(a.shape[0], a.shape[1], n_heads, -1)


def to_chunks(a, block):
    b_, t = a.shape[0], a.shape[1]
    a = a.reshape((b_, t // block, block) + a.shape[2:])
    return jnp.moveaxis(a, (1, 2), (0, 3))


def from_chunks(o):
    o = jnp.moveaxis(o, (0, 3), (1, 2))
    return o.reshape(o.shape[0], o.shape[1] * o.shape[2], o.shape[3], o.shape[4])


def hgrn2_chunked(q, k, v, log_f, s0):
    t = q.shape[1]
    blk = min(CHUNK, t)
    mask = jnp.tril(jnp.ones((blk, blk), dtype=bool))[None, None, :, :, None]

    def step(s, inp):
        qc, kc, vc, gc = inp
        g_cum = jnp.cumsum(gc, axis=2)
        diff = g_cum[:, :, :, None, :] - g_cum[:, :, None, :, :]
        decay = jnp.exp(jnp.where(mask, diff, -jnp.inf))
        att = jnp.einsum('bhtk,bhtsk,bhsk->bhts', qc, decay, kc)
        o = jnp.einsum('bhtk,bhkv->bhtv', qc * jnp.exp(g_cum), s) + jnp.einsum('bhts,bhsv->bhtv', att, vc)
        g_last = g_cum[:, :, -1:, :]
        s = jnp.exp(g_last[:, :, 0, :])[..., None] * s + jnp.einsum('bhsk,bhsv->bhkv', kc * jnp.exp(g_last - g_cum), vc)
        return s, o

    s_new, o = lax.scan(step, s0, (to_chunks(q, blk), to_chunks(k, blk), to_chunks(v, blk), to_chunks(log_f, blk)))
    return from_chunks(o), s_new


def mlstm_chunked(q, k, v, i_pre, log_f, c0, n0, m0):
    t = q.shape[1]
    blk = min(CHUNK, t)
    mask = jnp.tril(jnp.ones((blk, blk), dtype=bool))[None, None]

    def gate_chunks(a):
        return to_chunks(a[..., None], blk)[..., 0]

    def step(carry, inp):
        c, nv, m = carry
        qc, kc, vc, ic, fc = inp
        b = jnp.cumsum(fc, axis=-1)
        dlog = jnp.where(mask, b[..., :, None] - b[..., None, :] + ic[..., None, :], -jnp.inf)
        inter_log = b + m[..., None]
        m_t = jnp.maximum(inter_log, jnp.max(dlog, axis=-1))
        w = jnp.exp(dlog - m_t[..., None])
        inter_w = jnp.exp(inter_log - m_t)
        sc = jnp.einsum('bhtk,bhsk->bhts', qc, kc) * w
        num = inter_w[..., None] * jnp.einsum('bhvk,bhtk->bhtv', c, qc) + jnp.einsum('bhts,bhsv->bhtv', sc, vc)
        den = inter_w * jnp.einsum('bhk,bhtk->bht', nv, qc) + jnp.sum(sc, axis=-1)
        h = num / jnp.maximum(jnp.abs(den), jnp.exp(-m_t))[..., None]
        b_last = b[..., -1]
        wlog = b_last[..., None] - b + ic
        m_new = jnp.maximum(b_last + m, jnp.max(wlog, axis=-1))
        carry_decay = jnp.exp(b_last + m - m_new)
        wk = jnp.exp(wlog - m_new[..., None])
        c = carry_decay[..., None, None] * c + jnp.einsum('bhs,bhsv,bhsk->bhvk', wk, vc, kc)
        nv = carry_decay[..., None] * nv + jnp.einsum('bhs,bhsk->bhk', wk, kc)
        return (c, nv, m_new), h

    (c_new, n_new, m_new), h = lax.scan(
        step, (c0, n0, m0),
        (to_chunks(q, blk), to_chunks(k, blk), to_chunks(v, blk), gate_chunks(i_pre), gate_chunks(log_f)))
    return from_chunks(h), c_new, n_new, m_new


def causal_conv(x, buf, w, b):
    t = x.shape[1]
    xp = jnp.concatenate([buf.astype(x.dtype), x], axis=1)
    y = b
    for j in range(CONV_W):
        y = y + xp[:, j:j + t] * w[j]
    return y, xp[:, -(CONV_W - 1):]


def moe_ffn(u, w_router, router_bias, w_exp_gate, w_exp_up, w_exp_down, w_sh_gate, w_sh_up, w_sh_down):
    bt, t, d = u.shape
    x = u.reshape(-1, d)
    ntok = x.shape[0]
    scores = jax.nn.sigmoid((x @ w_router).astype(jnp.float32))
    biased = scores + router_bias.astype(jnp.float32)
    group_score = jnp.sum(lax.top_k(biased.reshape(ntok, N_GROUPS, -1), 2)[0], axis=-1)
    _, gidx = lax.top_k(group_score, TOPK_GROUPS)
    gmask = jnp.sum(jax.nn.one_hot(gidx, N_GROUPS), axis=1) > 0
    masked = jnp.where(jnp.repeat(gmask, N_EXPERTS // N_GROUPS, axis=1), biased, -jnp.inf)
    _, eidx = lax.top_k(masked, TOP_K)
    wsel = jnp.take_along_axis(scores, eidx, axis=1)
    wsel = wsel / jnp.sum(wsel, axis=-1, keepdims=True) * ROUTED_SCALE
    nrow = ntok * TOP_K
    flat_e = eidx.reshape(-1)
    flat_tok = jnp.repeat(jnp.arange(ntok, dtype=jnp.int32), TOP_K)
    flat_w = wsel.reshape(-1)
    order = jnp.argsort(flat_e)
    se, stok, sw = flat_e[order], flat_tok[order], flat_w[order]
    counts = jnp.bincount(flat_e, length=N_EXPERTS)
    padded = (counts + MOE_BLOCK - 1) // MOE_BLOCK * MOE_BLOCK
    start = jnp.cumsum(counts) - counts
    pend = jnp.cumsum(padded)
    pstart = pend - padded
    dest = pstart[se] + jnp.arange(nrow, dtype=jnp.int32) - start[se]
    n_blocks = -(-(nrow + N_EXPERTS * (MOE_BLOCK - 1)) // MOE_BLOCK)
    rows = n_blocks * MOE_BLOCK
    row_tok = jnp.full((rows,), ntok, jnp.int32).at[dest].set(stok)
    row_w = jnp.zeros((rows,), jnp.float32).at[dest].set(sw)
    block_e = jnp.minimum(jnp.searchsorted(pend, jnp.arange(n_blocks, dtype=pend.dtype) * MOE_BLOCK, side='right'), N_EXPERTS - 1)
    x_pad = jnp.concatenate([x, jnp.zeros((1, d), x.dtype)], axis=0)

    def expert_block(args):
        tok, e = args
        xb = x_pad[tok]
        return (jax.nn.silu(xb @ w_exp_gate[e]) * (xb @ w_exp_up[e])) @ w_exp_down[e]

    y_rows = lax.map(expert_block, (row_tok.reshape(n_blocks, MOE_BLOCK), block_e))
    routed = jnp.zeros((ntok + 1, d), jnp.float32).at[row_tok].add(
        y_rows.reshape(rows, d).astype(jnp.float32) * row_w[:, None])[:ntok]
    shared = (jax.nn.silu(x @ w_sh_gate) * (x @ w_sh_up)) @ w_sh_down
    return (routed.astype(x.dtype) + shared).reshape(bt, t, d)


def layer(x, c, s0, c0, n0, m0, conv0, lb, p):
    f32 = jnp.float32
    bt, t, _ = x.shape
    ada = (jax.nn.silu(c) @ p['w_ada'] + p['b_ada'])[:, None, :]
    shift1, scale1, gate1, shift2, scale2, gate2 = jnp.split(ada, 6, axis=-1)
    u = layer_norm(x) * (1 + scale1) + shift1
    z = u @ p['w_in'] + p['b_in']
    hq, hf, hi, hgate, mqk, mv, mo, mi, mf, ga, gb = split_cols(z)

    lb_h = lb.reshape(HG_HEADS, HG_DK)
    hf_h = to_heads(hf.astype(f32), HG_HEADS)
    log_f = jnp.log(lb_h + (1 - lb_h) * jax.nn.sigmoid(hf_h))
    key = (1 - lb_h) * jax.nn.sigmoid(-hf_h)
    o_h, s_new = hgrn2_chunked(to_heads(jax.nn.silu(hq.astype(f32)), HG_HEADS), key,
                               to_heads(hi.astype(f32), HG_HEADS), log_f, s0.astype(f32))
    o_h = o_h * lax.rsqrt(jnp.mean(jnp.square(o_h), axis=-1, keepdims=True) + LN_EPS) * p['w_hg_norm'].astype(f32)
    o_h = (o_h.reshape(bt, t, HG_WIDTH) * jax.nn.silu(hgate.astype(f32))).astype(x.dtype)

    qk, conv_new = causal_conv(mqk, conv0, p['w_conv'], p['b_conv'])
    mq, mk = jnp.split(jax.nn.silu(qk).astype(f32), 2, axis=-1)
    h_m, c_new, n_new, m_new = mlstm_chunked(
        to_heads(mq, ML_HEADS), to_heads(mk, ML_HEADS) * ML_DK ** -0.5, to_heads(mv.astype(f32), ML_HEADS),
        mi.astype(f32), jax.nn.log_sigmoid(mf.astype(f32)), c0.astype(f32), n0.astype(f32), m0.astype(f32))
    mu = jnp.mean(h_m, axis=-1, keepdims=True)
    h_m = (h_m - mu) * lax.rsqrt(jnp.mean(jnp.square(h_m - mu), axis=-1, keepdims=True) + LN_EPS)
    h_m = h_m * p['w_ml_norm'].astype(f32).reshape(ML_HEADS, ML_DV)
    h_m = (h_m.reshape(bt, t, ML_V_WIDTH) * jax.nn.sigmoid(mo.astype(f32))).astype(x.dtype)

    merged = jax.nn.sigmoid(ga) * (o_h @ p['w_proj_a']) + jax.nn.sigmoid(gb) * (h_m @ p['w_proj_b'])
    x = layer_norm(DEEPNORM_ALPHA * x + (1 + gate1) * (merged @ p['w_out']), p['ln1_w'], p['ln1_b'])

    u2 = layer_norm(x) * (1 + scale2) + shift2
    ff = moe_ffn(u2, p['w_router'], p['router_bias'], p['w_exp_gate'], p['w_exp_up'], p['w_exp_down'],
                 p['w_sh_gate'], p['w_sh_up'], p['w_sh_down'])
    x = layer_norm(DEEPNORM_ALPHA * x + (1 + gate2) * ff, p['ln2_w'], p['ln2_b'])
    return x, s_new, c_new, n_new, m_new, conv_new


def setup_inputs(seed: int = 0) -> dict:
    key = jax.random.key(seed)
    ks = jax.random.split(key, 40)
    f32 = jnp.float32

    def nrm(k, shape, scale):
        return jax.random.normal(k, shape, f32) * scale

    hd = D_MODEL ** -0.5
    f_off = sum(IN_SPLITS[:8])
    b_in = nrm(ks[5], (DEPTH, D_IN), 0.01)
    b_in = b_in.at[:, f_off:f_off + ML_HEADS].add(jnp.linspace(3.0, 6.0, ML_HEADS))
    lb_offset = jnp.concatenate([jnp.full((DEPTH, 1), -1.0, f32), jnp.full((1, 1), 1.0, f32)], axis=0)
    return {
        'x_prompt': nrm(ks[0], (BATCH, SEQ, D_MODEL), 1.0),
        'x_sample': nrm(ks[1], (DEC_BATCH, DEC_SEQ, D_MODEL), 1.0),
        'state_hgrn': nrm(ks[2], (DEPTH, DEC_BATCH, HG_HEADS, HG_DK, HG_DV), 0.5),
        'state_mlstm_c': nrm(ks[3], (DEPTH, DEC_BATCH, ML_HEADS, ML_DV, ML_DK), 0.5),
        'state_mlstm_n': nrm(ks[4], (DEPTH, DEC_BATCH, ML_HEADS, ML_DK), 0.5),
        'state_mlstm_m': nrm(ks[30], (DEPTH, DEC_BATCH, ML_HEADS), 1.0),
        'state_mlstm_conv': nrm(ks[31], (DEPTH, DEC_BATCH, CONV_W - 1, ML_QK_WIDTH), 1.0),
        'c_prompt': nrm(ks[32], (BATCH, D_MODEL), 1.0),
        'c_sample': nrm(ks[33], (DEC_BATCH, D_MODEL), 1.0),
        'w_ada': nrm(ks[6], (DEPTH, D_MODEL, 6 * D_MODEL), 0.02 * hd),
        'b_ada': nrm(ks[7], (DEPTH, 6 * D_MODEL), 0.01),
        'w_in': nrm(ks[8], (DEPTH, D_MODEL, D_IN), hd),
        'b_in': b_in,
        'lb_logits': nrm(ks[9], (DEPTH + 1, HG_WIDTH), 0.1) + lb_offset,
        'w_hg_norm': 1.0 + nrm(ks[10], (DEPTH, HG_DV), 0.01),
        'w_conv': nrm(ks[11], (DEPTH, CONV_W, ML_QK_WIDTH), CONV_W ** -0.5),
        'b_conv': nrm(ks[12], (DEPTH, ML_QK_WIDTH), 0.01),
        'w_ml_norm': 1.0 + nrm(ks[13], (DEPTH, ML_V_WIDTH), 0.01),
        'w_proj_a': nrm(ks[14], (DEPTH, HG_WIDTH, D_MODEL), HG_WIDTH ** -0.5),
        'w_proj_b': nrm(ks[15], (DEPTH, ML_V_WIDTH, D_MODEL), ML_V_WIDTH ** -0.5),
        'w_out': nrm(ks[16], (DEPTH, D_MODEL, D_MODEL), hd * DEEPNORM_BETA),
        'ln1_w': 1.0 + nrm(ks[17], (DEPTH, D_MODEL), 0.01),
        'ln1_b': nrm(ks[18], (DEPTH, D_MODEL), 0.01),
        'w_router': nrm(ks[19], (DEPTH, D_MODEL, N_EXPERTS), hd),
        'router_bias': nrm(ks[20], (DEPTH, N_EXPERTS), 0.01),
        'w_exp_gate': nrm(ks[21], (DEPTH, N_EXPERTS, D_MODEL, D_EXPERT), hd),
        'w_exp_up': nrm(ks[22], (DEPTH, N_EXPERTS, D_MODEL, D_EXPERT), hd),
        'w_exp_down': nrm(ks[23], (DEPTH, N_EXPERTS, D_EXPERT, D_MODEL), D_EXPERT ** -0.5 * DEEPNORM_BETA),
        'w_sh_gate': nrm(ks[24], (DEPTH, D_MODEL, D_SHARED), hd),
        'w_sh_up': nrm(ks[25], (DEPTH, D_MODEL, D_SHARED), hd),
        'w_sh_down': nrm(ks[26], (DEPTH, D_SHARED, D_MODEL), D_SHARED ** -0.5 * DEEPNORM_BETA),
        'ln2_w': 1.0 + nrm(ks[27], (DEPTH, D_MODEL), 0.01),
        'ln2_b': nrm(ks[28], (DEPTH, D_MODEL), 0.01),
    }


def reference(x_prompt, x_sample, state_hgrn, state_mlstm_c, state_mlstm_n, state_mlstm_m, state_mlstm_conv,
              c_prompt, c_sample, w_ada, b_ada, w_in, b_in, lb_logits, w_hg_norm, w_conv, b_conv, w_ml_norm,
              w_proj_a, w_proj_b, w_out, ln1_w, ln1_b, w_router, router_bias, w_exp_gate, w_exp_up, w_exp_down,
              w_sh_gate, w_sh_up, w_sh_down, ln2_w, ln2_b):
    f32 = jnp.float32
    lower_bounds = jnp.cumsum(jax.nn.softmax(lb_logits.astype(f32), axis=0), axis=0)
    bp = x_prompt.shape[0]
    s0_p = jnp.zeros((bp, HG_HEADS, HG_DK, HG_DV), f32)
    c0_p = jnp.zeros((bp, ML_HEADS, ML_DV, ML_DK), f32)
    n0_p = jnp.zeros((bp, ML_HEADS, ML_DK), f32)
    m0_p = jnp.zeros((bp, ML_HEADS), f32)
    conv0_p = jnp.zeros((bp, CONV_W - 1, ML_QK_WIDTH), x_prompt.dtype)
    yp, ys = x_prompt, x_sample
    hg_p, mc_p, mn_p, mm_p, cv_p = [], [], [], [], []
    hg_s, mc_s, mn_s, mm_s, cv_s = [], [], [], [], []
    for l in range(DEPTH):
        p = {
            'w_ada': w_ada[l], 'b_ada': b_ada[l], 'w_in': w_in[l], 'b_in': b_in[l],
            'w_hg_norm': w_hg_norm[l], 'w_conv': w_conv[l], 'b_conv': b_conv[l], 'w_ml_norm': w_ml_norm[l],
            'w_proj_a': w_proj_a[l], 'w_proj_b': w_proj_b[l], 'w_out': w_out[l],
            'ln1_w': ln1_w[l], 'ln1_b': ln1_b[l], 'w_router': w_router[l], 'router_bias': router_bias[l],
            'w_exp_gate': w_exp_gate[l], 'w_exp_up': w_exp_up[l], 'w_exp_down': w_exp_down[l],
            'w_sh_gate': w_sh_gate[l], 'w_sh_up': w_sh_up[l], 'w_sh_down': w_sh_down[l],
            'ln2_w': ln2_w[l], 'ln2_b': ln2_b[l],
        }
        yp, s_p, c_p, n_p, m_p, cb_p = layer(yp, c_prompt, s0_p, c0_p, n0_p, m0_p, conv0_p, lower_bounds[l], p)
        ys, s_s, c_s, n_s, m_s, cb_s = layer(ys, c_sample, state_hgrn[l], state_mlstm_c[l], state_mlstm_n[l],
                                             state_mlstm_m[l], state_mlstm_conv[l], lower_bounds[l], p)
        hg_p.append(s_p.astype(x_prompt.dtype))
        mc_p.append(c_p.astype(x_prompt.dtype))
        mn_p.append(n_p.astype(x_prompt.dtype))
        mm_p.append(m_p.astype(x_prompt.dtype))
        cv_p.append(cb_p.astype(x_prompt.dtype))
        hg_s.append(s_s.astype(state_hgrn.dtype))
        mc_s.append(c_s.astype(state_mlstm_c.dtype))
        mn_s.append(n_s.astype(state_mlstm_n.dtype))
        mm_s.append(m_s.astype(state_mlstm_m.dtype))
        cv_s.append(cb_s.astype(state_mlstm_conv.dtype))
    return (yp, ys,
            jnp.stack(hg_p, 0), jnp.stack(mc_p, 0), jnp.stack(mn_p, 0), jnp.stack(mm_p, 0), jnp.stack(cv_p, 0),
            jnp.stack(hg_s, 0), jnp.stack(mc_s, 0), jnp.stack(mn_s, 0), jnp.stack(mm_s, 0), jnp.stack(cv_s, 0))
```

```python
import functools

import jax
import jax.numpy as jnp
from jax import lax
from jax.experimental import pallas as pl
from jax.experimental.pallas import tpu as pltpu

F32, BF16, I32 = jnp.float32, jnp.bfloat16, jnp.int32
HIGHEST = lax.Precision.HIGHEST
NT_DIMS = (((1,), (1,)), ((), ()))
TN_DIMS = (((0,), (0,)), ((), ()))

CHUNK = 64
SUB = 16
HG_HEADS, HG_DK, HG_DV = 8, 128, 128
ML_HEADS, ML_DK, ML_DV = 4, 128, 256
CONV_W = 4
TOP_K, N_GROUPS, TOPK_GROUPS = 8, 8, 4
ROUTED_SCALE = 2.5
LN_EPS = 1e-5
HG_WIDTH = HG_HEADS * HG_DK
ML_QK_WIDTH = 2 * ML_HEADS * ML_DK
ML_V_WIDTH = ML_HEADS * ML_DV
COL = 1024
GATE_COLS = 256
EXP_CAP = 80.0
MOE_TILE = 256
MIB = 1024 * 1024
VMEM_LIMIT = 56 * MIB


def _cp(n_axes, vmem=VMEM_LIMIT):
    return pltpu.CompilerParams(dimension_semantics=("arbitrary",) * n_axes, vmem_limit_bytes=vmem)


def _sds(shape, dtype):
    return jax.ShapeDtypeStruct(shape, dtype)


def _const_spec(shape):
    n = len(shape)
    return pl.BlockSpec(shape, lambda *_: (0,) * n, pipeline_mode=pl.Buffered(1))


def _sigmoid(x):
    return jax.nn.sigmoid(x)


def _silu(x):
    return x * jax.nn.sigmoid(x)


def _ln(x):
    mu = jnp.mean(x, axis=-1, keepdims=True)
    xc = x - mu
    var = jnp.mean(xc * xc, axis=-1, keepdims=True)
    return xc * lax.rsqrt(var + LN_EPS)


def _dot(a, b, **kw):
    return jnp.dot(a, b, preferred_element_type=F32, **kw)


def _dot_nt(a, b, **kw):
    return lax.dot_general(a, b, NT_DIMS, preferred_element_type=F32, **kw)


def _dot_tn(a, b, **kw):
    return lax.dot_general(a, b, TN_DIMS, preferred_element_type=F32, **kw)


def _ada_kernel(c_ref, w_ref, b_ref, o_ref):
    a = _silu(c_ref[...]).astype(BF16)
    o_ref[...] = _dot(a, w_ref[...].astype(BF16)) + b_ref[...]


def _ada(c_all, w_ada, b_ada):
    r, d = c_all.shape
    n6 = w_ada.shape[1]
    tn = 1024
    return pl.pallas_call(
        _ada_kernel,
        out_shape=_sds((r, n6), F32),
        grid=(n6 // tn,),
        in_specs=[pl.BlockSpec((r, d), lambda j: (0, 0)),
                  pl.BlockSpec((d, tn), lambda j: (0, j)),
                  pl.BlockSpec((1, tn), lambda j: (0, j))],
        out_specs=pl.BlockSpec((r, tn), lambda j: (0, j)),
        compiler_params=_cp(1),
        name="ada",
    )(c_all, w_ada, b_ada)


def _stage_a_kernel(x_ref, sc_ref, sh_ref, w_ref, b_ref, wg_ref, bg_ref, z_ref, g_ref, u_scr):
    @pl.when(pl.program_id(1) == 0)
    def _():
        u = _ln(x_ref[...]) * (1.0 + sc_ref[...]) + sh_ref[...]
        u2d = u.reshape(u_scr.shape).astype(BF16)
        u_scr[...] = u2d
        g_ref[...] = _dot(u2d, wg_ref[...]) + bg_ref[...]

    z_ref[...] = _dot(u_scr[...], w_ref[...]) + b_ref[...]


def _stage_a(x3, ada3, w_main, b_main, w_gate, b_gate, bb, tt):
    b, t, d = x3.shape
    nt = t // tt
    tm = bb * tt
    ncol = w_main.shape[1] // COL
    n = b * t
    return pl.pallas_call(
        _stage_a_kernel,
        out_shape=(_sds((n, w_main.shape[1]), F32), _sds((n, GATE_COLS), F32)),
        grid=((b // bb) * nt, ncol),
        in_specs=[pl.BlockSpec((bb, tt, d), lambda i, j: (i // nt, i % nt, 0)),
                  pl.BlockSpec((bb, 1, d), lambda i, j: (i // nt, 0, 1)),
                  pl.BlockSpec((bb, 1, d), lambda i, j: (i // nt, 0, 0)),
                  pl.BlockSpec((d, COL), lambda i, j: (0, j)),
                  pl.BlockSpec((1, COL), lambda i, j: (0, j)),
                  _const_spec((d, GATE_COLS)),
                  _const_spec((1, GATE_COLS))],
        out_specs=(pl.BlockSpec((tm, COL), lambda i, j: (i, j)),
                   pl.BlockSpec((tm, GATE_COLS), lambda i, j: (i, 0))),
        scratch_shapes=[pltpu.VMEM((tm, d), BF16)],
        compiler_params=_cp(2),
        name="stage_a",
    )(x3, ada3, ada3, w_main, b_main, w_gate, b_gate)


ZC_HQ, ZC_HF, ZC_HI, ZC_HG, ZC_MQK, ZC_MV, ZC_MO = range(7)


def _zspec(tt, nt, d_model, k):
    base = 2 * d_model // COL
    return pl.BlockSpec((tt, COL), lambda bi, j: (bi * nt + j, base + k))


def _hgrn_kernel(hq_ref, hf_ref, hi_ref, hg_ref, lbl_ref, wn_ref, s0_ref, o_ref, sout_ref, st_scr, *, L, nc):
    j = pl.program_id(1)
    heads = st_scr.shape[0]

    @pl.when(j == 0)
    def _():
        for h in range(heads):
            st_scr[h] = s0_ref[0, h].T

    lbl = lbl_ref[...]
    e = jnp.exp(lbl - jnp.max(lbl, axis=0, keepdims=True))
    lb_all = e[0:1] / jnp.sum(e, axis=0, keepdims=True)
    row = lax.broadcasted_iota(I32, (L, L), 0)
    col = lax.broadcasted_iota(I32, (L, L), 1)
    causal = row >= col
    tri = causal.astype(F32)
    wn = wn_ref[...]

    def chunk(c, carry):
        r0 = pl.multiple_of(c * L, L)
        rows = pl.ds(r0, L)
        for h in range(heads):
            lo, hi_ = h * HG_DK, (h + 1) * HG_DK
            hq = hq_ref[rows, lo:hi_]
            hf = hf_ref[rows, lo:hi_]
            v = hi_ref[rows, lo:hi_]
            hg = hg_ref[rows, lo:hi_]
            lb = lb_all[:, lo:hi_]
            logf = jnp.log(lb + (1.0 - lb) * _sigmoid(hf))
            key = (1.0 - lb) * _sigmoid(-hf)
            q = _silu(hq)
            g = _dot(tri, logf, precision=HIGHEST)
            g_last = g[L - 1:L, :]
            st = st_scr[h]
            vb = v.astype(BF16)
            o = _dot_nt((q * jnp.exp(g)).astype(BF16), st.astype(BF16))
            blocks = []
            for i in range(L // SUB):
                gref = g[i * SUB + SUB // 2:i * SUB + SUB // 2 + 1, :]
                qi = (q[i * SUB:(i + 1) * SUB] * jnp.exp(g[i * SUB:(i + 1) * SUB] - gref)).astype(BF16)
                kj = (key * jnp.exp(jnp.minimum(gref - g, EXP_CAP))).astype(BF16)
                blocks.append(_dot_nt(qi, kj))
            att = blocks[0] if len(blocks) == 1 else jnp.concatenate(blocks, axis=0)
            att = jnp.where(causal, att, 0.0)
            o = o + _dot(att.astype(BF16), vb)
            khat = (key * jnp.exp(g_last - g)).astype(BF16)
            st_scr[h] = st * jnp.exp(g_last) + _dot_tn(vb, khat)
            o = o * lax.rsqrt(jnp.mean(o * o, axis=-1, keepdims=True) + LN_EPS) * wn
            o_ref[rows, lo:hi_] = (o * _silu(hg)).astype(o_ref.dtype)
        return carry

    lax.fori_loop(0, nc, chunk, 0)

    @pl.when(j == pl.num_programs(1) - 1)
    def _():
        for h in range(heads):
            sout_ref[0, h] = st_scr[h].T


def _hgrn(z, lb_logits, w_norm, s0, b, t, tt, d_model):
    L = min(CHUNK, t)
    nt = t // tt
    zspec = functools.partial(_zspec, tt, nt, d_model)
    return pl.pallas_call(
        functools.partial(_hgrn_kernel, L=L, nc=tt // L),
        out_shape=(_sds((b * t, HG_WIDTH), BF16), _sds(s0.shape, F32)),
        grid=(b, nt),
        in_specs=[zspec(ZC_HQ), zspec(ZC_HF), zspec(ZC_HI), zspec(ZC_HG),
                  _const_spec(lb_logits.shape), _const_spec(w_norm.shape),
                  pl.BlockSpec((1,) + s0.shape[1:], lambda bi, j: (bi, 0, 0, 0))],
        out_specs=(pl.BlockSpec((tt, HG_WIDTH), lambda bi, j: (bi * nt + j, 0)),
                   pl.BlockSpec((1,) + s0.shape[1:], lambda bi, j: (bi, 0, 0, 0))),
        scratch_shapes=[pltpu.VMEM((HG_HEADS, HG_DV, HG_DK), F32)],
        compiler_params=_cp(2),
        name="hgrn",
    )(z, z, z, z, lb_logits, w_norm, s0)


def _mlstm_kernel(qk_ref, v_ref, mo_ref, g_ref, conv0_ref, wc_ref, bc_ref, wn_ref, c0_ref, n0_ref, m0_ref,
                  o_ref, cout_ref, nout_ref, mout_ref, convout_ref,
                  xbuf, qk_scr, c_scr, n_scr, m_scr, *, L, nc):
    j = pl.program_id(1)
    last = pl.num_programs(1) - 1
    heads = c_scr.shape[0]
    tt = qk_ref.shape[0]
    P = CONV_W - 1
    OFF = 8
    QW = heads * ML_DK

    @pl.when(j == 0)
    def _():
        xbuf[OFF - P:OFF, :] = conv0_ref[0]
        c_scr[...] = c0_ref[0]
        n_scr[0:heads, :] = n0_ref[0]
        m0 = m0_ref[0]
        for h in range(heads):
            m_scr[h:h + 1, :] = jnp.broadcast_to(m0[:, h:h + 1], (1, 128))

    xbuf[OFF:OFF + tt, :] = qk_ref[...]
    acc = bc_ref[...] + wc_ref[0:1, :] * xbuf[OFF - P:OFF - P + tt, :]
    for tap in range(1, CONV_W):
        acc = acc + wc_ref[tap:tap + 1, :] * xbuf[OFF - P + tap:OFF - P + tap + tt, :]
    qk = _silu(acc)
    qk_scr[:, 0:QW] = qk[:, 0:QW]
    qk_scr[:, QW:2 * QW] = qk[:, QW:2 * QW] * (ML_DK ** -0.5)

    @pl.when(j == last)
    def _():
        convout_ref[0] = xbuf[OFF + tt - P:OFF + tt, :]

    xbuf[OFF - P:OFF, :] = xbuf[OFF + tt - P:OFF + tt, :]

    row = lax.broadcasted_iota(I32, (L, L), 0)
    col = lax.broadcasted_iota(I32, (L, L), 1)
    causal = row >= col
    tri = causal.astype(F32)
    wn = wn_ref[...]

    def chunk(c, carry):
        r0 = pl.multiple_of(c * L, L)
        rows = pl.ds(r0, L)
        gates = g_ref[rows, :]
        gi = gates[:, 0:128]
        gf = gates[:, 128:256]
        lf = jnp.minimum(gf, 0.0) - jnp.log(1.0 + jnp.exp(-jnp.abs(gf)))
        bcum = _dot(tri, lf, precision=HIGHEST)
        a = gi - bcum
        a_t = a.T
        for h in range(heads):
            q = qk_scr[rows, h * ML_DK:(h + 1) * ML_DK]
            k = qk_scr[rows, QW + h * ML_DK:QW + (h + 1) * ML_DK]
            v = v_ref[rows, h * ML_DV:(h + 1) * ML_DV]
            b_col = bcum[:, h:h + 1]
            a_col = a[:, h:h + 1]
            a_row = a_t[h:h + 1, :]
            m_prev = m_scr[h:h + 1, 0:1]
            dlog = jnp.where(causal, b_col + a_row, -jnp.inf)
            inter_log = b_col + m_prev
            m_t = jnp.maximum(inter_log, jnp.max(dlog, axis=-1, keepdims=True))
            w = jnp.exp(dlog - m_t)
            inter_w = jnp.exp(inter_log - m_t)
            qb, kb, vb = q.astype(BF16), k.astype(BF16), v.astype(BF16)
            sc = _dot_nt(qb, kb) * w
            cst = c_scr[h]
            num = inter_w * _dot_nt(qb, cst.astype(BF16)) + _dot(sc.astype(BF16), vb)
            nrow = n_scr[h:h + 1, :]
            den = inter_w * jnp.sum(q * nrow, axis=-1, keepdims=True) + jnp.sum(sc, axis=-1, keepdims=True)
            hh = num / jnp.maximum(jnp.abs(den), jnp.exp(-m_t))
            b_last = b_col[L - 1:L, :]
            m_new = m_t[L - 1:L, :]
            decay = jnp.exp(b_last + m_prev - m_new)
            kw = k * jnp.exp(b_last + a_col - m_new)
            c_scr[h] = decay * cst + _dot_tn(vb, kw.astype(BF16))
            n_scr[h:h + 1, :] = decay * nrow + jnp.sum(kw, axis=0, keepdims=True)
            m_scr[h:h + 1, :] = jnp.broadcast_to(m_new, (1, 128))
            mu = jnp.mean(hh, axis=-1, keepdims=True)
            hc = hh - mu
            hn = hc * lax.rsqrt(jnp.mean(hc * hc, axis=-1, keepdims=True) + LN_EPS)
            hn = hn * wn[:, h * ML_DV:(h + 1) * ML_DV]
            og = _sigmoid(mo_ref[rows, h * ML_DV:(h + 1) * ML_DV])
            o_ref[rows, h * ML_DV:(h + 1) * ML_DV] = (hn * og).astype(o_ref.dtype)
        return carry

    lax.fori_loop(0, nc, chunk, 0)

    @pl.when(j == last)
    def _():
        cout_ref[0] = c_scr[...]
        nout_ref[0] = n_scr[0:heads, :]
        lane = lax.broadcasted_iota(I32, (1, 128), 1)
        mrow = jnp.zeros((1, 128), F32)
        for h in range(heads):
            mrow = jnp.where(lane == h, m_scr[h:h + 1, :], mrow)
        mout_ref[0] = mrow[:, 0:heads]


def _mlstm(z, gates, conv0, w_conv, b_conv, w_norm, c0, n0, m0, b, t, tt, d_model):
    L = min(CHUNK, t)
    nt = t // tt
    m0 = m0.reshape(b, 1, ML_HEADS)
    zspec = functools.partial(_zspec, tt, nt, d_model)
    bspec = lambda a: pl.BlockSpec((1,) + a.shape[1:], lambda bi, j: (bi,) + (0,) * (a.ndim - 1))
    outs = pl.pallas_call(
        functools.partial(_mlstm_kernel, L=L, nc=tt // L),
        out_shape=(_sds((b * t, ML_V_WIDTH), BF16), _sds(c0.shape, F32), _sds(n0.shape, F32),
                   _sds(m0.shape, F32), _sds(conv0.shape, F32)),
        grid=(b, nt),
        in_specs=[zspec(ZC_MQK), zspec(ZC_MV), zspec(ZC_MO),
                  pl.BlockSpec((tt, GATE_COLS), lambda bi, j: (bi * nt + j, 0)),
                  bspec(conv0), _const_spec(w_conv.shape), _const_spec(b_conv.shape), _const_spec(w_norm.shape),
                  bspec(c0), bspec(n0), bspec(m0)],
        out_specs=(pl.BlockSpec((tt, ML_V_WIDTH), lambda bi, j: (bi * nt + j, 0)),
                   bspec(c0), bspec(n0), bspec(m0), bspec(conv0)),
        scratch_shapes=[pltpu.VMEM((tt + 8, ML_QK_WIDTH), F32), pltpu.VMEM((tt, ML_QK_WIDTH), F32),
                        pltpu.VMEM((ML_HEADS, ML_DV, ML_DK), F32), pltpu.VMEM((8, 128), F32),
                        pltpu.VMEM((8, 128), F32)],
        compiler_params=_cp(2),
        name="mlstm",
    )(z, z, z, gates, conv0, w_conv, b_conv, w_norm, c0, n0, m0)
    h_m, c_new, n_new, m_new, conv_new = outs
    return h_m, c_new, n_new, m_new.reshape(b, ML_HEADS), conv_new


def _c1_kernel(x_ref, oh_ref, hm_ref, ga_ref, gb_ref, g1_ref, wpa_ref, wpb_ref, wo_ref, l1w_ref, l1b_ref,
               x1_ref, *, alpha):
    a = _dot(oh_ref[...], wpa_ref[...])
    b = _dot(hm_ref[...], wpb_ref[...])
    merged = _sigmoid(ga_ref[...]) * a + _sigmoid(gb_ref[...]) * b
    y = _dot(merged.astype(BF16), wo_ref[...])
    x = x_ref[...]
    t = alpha * x + (1.0 + g1_ref[...]) * y.reshape(x.shape)
    x1_ref[...] = _ln(t) * l1w_ref[...] + l1b_ref[...]


def _c1(x3, o_h, h_m, z, ada3, wpa, wpb, wo, l1w, l1b, bb, tt, alpha):
    b, t, d = x3.shape
    nt = t // tt
    tm = bb * tt
    xspec = pl.BlockSpec((bb, tt, d), lambda i: (i // nt, i % nt, 0))
    return pl.pallas_call(
        functools.partial(_c1_kernel, alpha=alpha),
        out_shape=_sds(x3.shape, F32),
        grid=((b // bb) * nt,),
        in_specs=[xspec,
                  pl.BlockSpec((tm, HG_WIDTH), lambda i: (i, 0)),
                  pl.BlockSpec((tm, ML_V_WIDTH), lambda i: (i, 0)),
                  pl.BlockSpec((tm, d), lambda i: (i, 0)),
                  pl.BlockSpec((tm, d), lambda i: (i, 1)),
                  pl.BlockSpec((bb, 1, d), lambda i: (i // nt, 0, 2)),
                  _const_spec(wpa.shape), _const_spec(wpb.shape), _const_spec(wo.shape),
                  _const_spec(l1w.shape), _const_spec(l1b.shape)],
        out_specs=xspec,
        compiler_params=_cp(1),
        name="c1",
    )(x3, o_h, h_m, z, z, ada3, wpa, wpb, wo, l1w, l1b)


def _c2_kernel(x1_ref, sh2_ref, sc2_ref, g2_ref, wrt_ref, wsg_ref, wsu_ref, wsd_ref, *rest, alpha):
    u2_ref, lt_ref, pre_ref = rest[-3:]
    x1 = x1_ref[...]
    u = _ln(x1) * (1.0 + sc2_ref[...]) + sh2_ref[...]
    u2d = u.reshape(u2_ref.shape)
    u2_ref[...] = u2d
    lt_ref[...] = _dot_nt(wrt_ref[...], u2d, precision=HIGHEST)
    ub = u2d.astype(BF16)
    hg = _dot(ub, wsg_ref[...])
    hu = _dot(ub, wsu_ref[...])
    shared = _dot((_silu(hg) * hu).astype(BF16), wsd_ref[...])
    pre_ref[...] = alpha * x1 + (1.0 + g2_ref[...]) * shared.reshape(x1.shape)


def _c2(x1, ada3, wrt, wsg, wsu, wsd, bb, tt, alpha, n_total, row0, shared_bufs):
    b, t, d = x1.shape
    nt = t // tt
    tm = bb * tt
    ne = wrt.shape[0]
    off = row0 // tm
    xspec = pl.BlockSpec((bb, tt, d), lambda i: (i // nt, i % nt, 0))
    mod = lambda c: pl.BlockSpec((bb, 1, d), lambda i: (i // nt, 0, c))
    in_specs = [xspec, mod(3), mod(4), mod(5),
                _const_spec(wrt.shape), _const_spec(wsg.shape), _const_spec(wsu.shape), _const_spec(wsd.shape)]
    args = [x1, ada3, ada3, ada3, wrt, wsg, wsu, wsd]
    aliases = {}
    if shared_bufs is not None:
        in_specs += [pl.BlockSpec(memory_space=pl.ANY), pl.BlockSpec(memory_space=pl.ANY)]
        aliases = {len(args): 0, len(args) + 1: 1}
        args += list(shared_bufs)
    return pl.pallas_call(
        functools.partial(_c2_kernel, alpha=alpha),
        out_shape=(_sds((n_total, d), F32), _sds((ne, n_total), F32), _sds(x1.shape, F32)),
        grid=((b // bb) * nt,),
        in_specs=in_specs,
        out_specs=(pl.BlockSpec((tm, d), lambda i: (off + i, 0)),
                   pl.BlockSpec((ne, tm), lambda i: (0, off + i)),
                   xspec),
        input_output_aliases=aliases,
        compiler_params=_cp(1),
        name="c2",
    )(*args)


def _route_kernel(lt_ref, bias_ref, eidx_ref, w_ref, rank_ref, cnt_ref, carry_scr):
    @pl.when(pl.program_id(0) == 0)
    def _():
        carry_scr[...] = jnp.zeros_like(carry_scr)

    ne, tr = lt_ref.shape
    gsz = ne // N_GROUPS
    s = _sigmoid(lt_ref[...])
    biased = s + bias_ref[...]
    neg = -jnp.inf
    blocks, gscore = [], []
    gio = lax.broadcasted_iota(I32, (gsz, tr), 0)
    for g in range(N_GROUPS):
        blk = biased[g * gsz:(g + 1) * gsz, :]
        m1 = jnp.max(blk, axis=0, keepdims=True)
        i1 = jnp.min(jnp.where(blk == m1, gio, gsz), axis=0, keepdims=True)
        m2 = jnp.max(jnp.where(gio == i1, neg, blk), axis=0, keepdims=True)
        blocks.append(blk)
        gscore.append(m1 + m2)
    masked = []
    for g in range(N_GROUPS):
        beat = jnp.zeros((1, tr), I32)
        for o in range(N_GROUPS):
            if o == g:
                continue
            wins = (gscore[o] >= gscore[g]) if o < g else (gscore[o] > gscore[g])
            beat = beat + wins.astype(I32)
        masked.append(jnp.where(beat < TOPK_GROUPS, blocks[g], neg))
    masked = jnp.concatenate(masked, axis=0)
    eio = lax.broadcasted_iota(I32, (ne, tr), 0)
    multihot = jnp.zeros((ne, tr), F32)
    idxs, ws = [], []
    for _ in range(TOP_K):
        mx = jnp.max(masked, axis=0, keepdims=True)
        idx = jnp.min(jnp.where(masked == mx, eio, ne), axis=0, keepdims=True)
        hit = eio == idx
        ws.append(jnp.sum(jnp.where(hit, s, 0.0), axis=0, keepdims=True))
        idxs.append(idx)
        masked = jnp.where(hit, neg, masked)
        multihot = multihot + hit.astype(F32)
    wsum = ws[0]
    for k in range(1, TOP_K):
        wsum = wsum + ws[k]
    tr_r = lax.broadcasted_iota(I32, (tr, tr), 0)
    tr_c = lax.broadcasted_iota(I32, (tr, tr), 1)
    before = (tr_r < tr_c).astype(BF16)
    ranks = _dot(multihot.astype(BF16), before) + carry_scr[:, 0:1]
    for k in range(TOP_K):
        eidx_ref[k:k + 1, :] = idxs[k]
        w_ref[k:k + 1, :] = ws[k] / wsum * ROUTED_SCALE
        rank_ref[k:k + 1, :] = jnp.sum(jnp.where(eio == idxs[k], ranks, 0.0), axis=0, keepdims=True).astype(I32)
    carry_scr[...] = carry_scr[...] + jnp.sum(multihot, axis=1, keepdims=True)
    cnt_ref[...] = carry_scr[...]


def _route(lt, bias_col, tr):
    ne, n = lt.shape
    kspec = pl.BlockSpec((TOP_K, tr), lambda i: (0, i))
    return pl.pallas_call(
        _route_kernel,
        out_shape=(_sds((TOP_K, n), I32), _sds((TOP_K, n), F32), _sds((TOP_K, n), I32), _sds((ne, 128), F32)),
        grid=(n // tr,),
        in_specs=[pl.BlockSpec((ne, tr), lambda i: (0, i)), _const_spec(bias_col.shape)],
        out_specs=(kspec, kspec, kspec, pl.BlockSpec((ne, 128), lambda i: (0, 0))),
        scratch_shapes=[pltpu.VMEM((ne, 128), F32)],
        compiler_params=_cp(1),
        name="route",
    )(lt, bias_col)


def _row_copy(src, src_row, dst, dst_row, sem):
    return pltpu.make_async_copy(src.at[pl.ds(src_row, 1)], dst.at[pl.ds(dst_row, 1)], sem)


def _dispatch_kernel(dest_ref, u2_hbm, xs_hbm, sem, *, td):
    base = pl.program_id(0) * td

    def issue(t, carry):
        for k in range(TOP_K):
            _row_copy(u2_hbm, base + t, xs_hbm, dest_ref[k, t], sem).start()
        return carry

    lax.fori_loop(0, td, issue, 0)
    for _ in range(TOP_K):
        pltpu.make_async_copy(u2_hbm.at[pl.ds(0, td)], xs_hbm.at[pl.ds(0, td)], sem).wait()


def _dispatch(dest, u2, rows_max, td):
    n, d = u2.shape
    return pl.pallas_call(
        functools.partial(_dispatch_kernel, td=td),
        out_shape=_sds((rows_max, d), u2.dtype),
        grid=(n // td,),
        in_specs=[pl.BlockSpec((TOP_K, td), lambda i: (0, i), memory_space=pltpu.SMEM),
                  pl.BlockSpec(memory_space=pl.ANY)],
        out_specs=pl.BlockSpec(memory_space=pl.ANY),
        scratch_shapes=[pltpu.SemaphoreType.DMA],
        compiler_params=_cp(1),
        name="dispatch",
    )(dest, u2)


def _expert_kernel(te_ref, nv_ref, xs_ref, wg_ref, wu_ref, wd_ref, y_ref, wgb, wub, wdb):
    j = pl.program_id(0)
    e = te_ref[j]
    prev = te_ref[jnp.maximum(j - 1, 0)]

    @pl.when((j == 0) | (e != prev))
    def _():
        wgb[...] = wg_ref[0].astype(BF16)
        wub[...] = wu_ref[0].astype(BF16)
        wdb[...] = wd_ref[0].astype(BF16)

    nv = nv_ref[j]

    @pl.when(nv > 0)
    def _():
        rows = lax.broadcasted_iota(I32, (xs_ref.shape[0], 1), 0)
        x = jnp.where(rows < nv, xs_ref[...], 0.0).astype(BF16)
        hg = _dot(x, wgb[...])
        hu = _dot(x, wub[...])
        y_ref[...] = _dot((_silu(hg) * hu).astype(BF16), wdb[...])


def _experts(tile_e, tile_nv, xs, w_gate, w_up, w_down):
    rows, d = xs.shape
    ne, _, f = w_gate.shape
    tm = MOE_TILE
    grid_spec = pltpu.PrefetchScalarGridSpec(
        num_scalar_prefetch=2,
        grid=(rows // tm,),
        in_specs=[pl.BlockSpec((tm, d), lambda j, te, nv: (j, 0)),
                  pl.BlockSpec((1, d, f), lambda j, te, nv: (te[j], 0, 0)),
                  pl.BlockSpec((1, d, f), lambda j, te, nv: (te[j], 0, 0)),
                  pl.BlockSpec((1, f, d), lambda j, te, nv: (te[j], 0, 0))],
        out_specs=pl.BlockSpec((tm, d), lambda j, te, nv: (j, 0)),
        scratch_shapes=[pltpu.VMEM((d, f), BF16), pltpu.VMEM((d, f), BF16), pltpu.VMEM((f, d), BF16)],
    )
    return pl.pallas_call(
        _expert_kernel,
        out_shape=_sds((rows, d), F32),
        grid_spec=grid_spec,
        compiler_params=_cp(1),
        name="experts",
    )(tile_e, tile_nv, xs, w_gate, w_up, w_down)


def _final_kernel(dest_ref, pre_ref, g2_ref, wt_ref, l2w_ref, l2b_ref, ys_hbm, out_ref, buf, sem):
    tm = wt_ref.shape[0]

    def issue(t, carry):
        for k in range(TOP_K):
            pltpu.make_async_copy(ys_hbm.at[pl.ds(dest_ref[k, t], 1)], buf.at[k, pl.ds(t, 1)], sem).start()
        return carry

    lax.fori_loop(0, tm, issue, 0)
    for k in range(TOP_K):
        pltpu.make_async_copy(ys_hbm.at[pl.ds(0, tm)], buf.at[k], sem).wait()
    wt = wt_ref[...]
    routed = buf[0] * wt[:, 0:1]
    for k in range(1, TOP_K):
        routed = routed + buf[k] * wt[:, k:k + 1]
    pre = pre_ref[...]
    t = pre + (1.0 + g2_ref[...]) * routed.reshape(pre.shape)
    out_ref[...] = _ln(t) * l2w_ref[...] + l2b_ref[...]


def _final(dest, pre, ada3, wt, l2w, l2b, ys, bb, tt, row0):
    b, t, d = pre.shape
    nt = t // tt
    tm = bb * tt
    off = row0 // tm
    xspec = pl.BlockSpec((bb, tt, d), lambda i: (i // nt, i % nt, 0))
    return pl.pallas_call(
        _final_kernel,
        out_shape=_sds(pre.shape, F32),
        grid=((b // bb) * nt,),
        in_specs=[pl.BlockSpec((TOP_K, tm), lambda i: (0, off + i), memory_space=pltpu.SMEM),
                  xspec,
                  pl.BlockSpec((bb, 1, d), lambda i: (i // nt, 0, 5)),
                  pl.BlockSpec((tm, TOP_K), lambda i: (off + i, 0)),
                  _const_spec(l2w.shape), _const_spec(l2b.shape),
                  pl.BlockSpec(memory_space=pl.ANY)],
        out_specs=xspec,
        scratch_shapes=[pltpu.VMEM((TOP_K, tm, d), F32), pltpu.SemaphoreType.DMA],
        compiler_params=_cp(1),
        name="final",
    )(dest, pre, ada3, wt, l2w, l2b, ys)


def _tile_t(t, cap):
    return min(t, cap)


def kernel(x_prompt, x_sample, state_hgrn, state_mlstm_c, state_mlstm_n, state_mlstm_m, state_mlstm_conv, c_prompt, c_sample, w_ada, b_ada, w_in, b_in, lb_logits, w_hg_norm, w_conv, b_conv, w_ml_norm, w_proj_a, w_proj_b, w_out, ln1_w, ln1_b, w_router, router_bias, w_exp_gate, w_exp_up, w_exp_down, w_sh_gate, w_sh_up, w_sh_down, ln2_w, ln2_b):
    depth = w_in.shape[0]
    assert depth == 1, "single-layer trunk"
    alpha = (2.0 * depth) ** 0.25
    bp, tp, d = x_prompt.shape
    bs, ts, _ = x_sample.shape
    n_p, n_s = bp * tp, bs * ts
    n_total = n_p + n_s
    ne = w_router.shape[-1]

    w = w_in[0]
    o_hg, o_mqk, o_mv, o_mo = 0, 4 * HG_WIDTH, 4 * HG_WIDTH + ML_QK_WIDTH, 4 * HG_WIDTH + ML_QK_WIDTH + ML_V_WIDTH
    o_mi = o_mo + ML_V_WIDTH
    o_mf = o_mi + ML_HEADS
    o_ga = o_mf + ML_HEADS
    o_gb = o_ga + d
    regroup = lambda a: jnp.concatenate([a[..., o_ga:o_gb + d], a[..., o_hg:o_mi]], axis=-1)
    w_main = regroup(w).astype(BF16)
    b_main = regroup(b_in[0])[None, :]
    pad = lambda a: jnp.pad(a, [(0, 0)] * (a.ndim - 1) + [(0, 128 - ML_HEADS)])
    gate_cols = lambda a: jnp.concatenate([pad(a[..., o_mi:o_mf]), pad(a[..., o_mf:o_ga])], axis=-1)
    w_gate = gate_cols(w).astype(BF16)
    b_gate = gate_cols(b_in[0])[None, :]
    wpa, wpb, wo = w_proj_a[0].astype(BF16), w_proj_b[0].astype(BF16), w_out[0].astype(BF16)
    wrt = w_router[0].T
    wsg, wsu, wsd = w_sh_gate[0].astype(BF16), w_sh_up[0].astype(BF16), w_sh_down[0].astype(BF16)
    row2 = lambda a: a[0][None, :]

    n_c = bp + bs
    c_all = jnp.pad(jnp.concatenate([c_prompt, c_sample], axis=0), [(0, -n_c % 8), (0, 0)])
    ada = _ada(c_all, w_ada[0], row2(b_ada))
    ada_p = ada[:bp][:, None, :]
    ada_s = ada[bp:n_c][:, None, :]

    def mixers(x3, ada3, s0, c0, n0, m0, conv0, bb, tt, tt_rec):
        b, t, _ = x3.shape
        z, gates = _stage_a(x3, ada3, w_main, b_main, w_gate, b_gate, bb, tt)
        o_h, s_new = _hgrn(z, lb_logits, w_hg_norm, s0, b, t, tt_rec, d)
        h_m, c_new, n_new, m_new, conv_new = _mlstm(z, gates, conv0, w_conv[0], row2(b_conv), row2(w_ml_norm),
                                                    c0, n0, m0, b, t, tt_rec, d)
        return z, o_h, h_m, (s_new, c_new, n_new, m_new, conv_new)

    zeros = lambda *s: jnp.zeros(s, F32)
    z_p, oh_p, hm_p, st_p = mixers(x_prompt, ada_p, zeros(bp, HG_HEADS, HG_DK, HG_DV), zeros(bp, ML_HEADS, ML_DV, ML_DK),
                                   zeros(bp, ML_HEADS, ML_DK), zeros(bp, ML_HEADS), zeros(bp, CONV_W - 1, ML_QK_WIDTH),
                                   1, _tile_t(tp, 1024), _tile_t(tp, 512))
    z_s, oh_s, hm_s, st_s = mixers(x_sample, ada_s, state_hgrn[0], state_mlstm_c[0], state_mlstm_n[0],
                                   state_mlstm_m[0], state_mlstm_conv[0], bs, ts, ts)

    tc_p = _tile_t(tp, 256)
    x1_p = _c1(x_prompt, oh_p, hm_p, z_p, ada_p, wpa, wpb, wo, row2(ln1_w), row2(ln1_b), 1, tc_p, alpha)
    x1_s = _c1(x_sample, oh_s, hm_s, z_s, ada_s, wpa, wpb, wo, row2(ln1_w), row2(ln1_b), bs, ts, alpha)
    u2, lt, pre_p = _c2(x1_p, ada_p, wrt, wsg, wsu, wsd, 1, tc_p, alpha, n_total, 0, None)
    u2, lt, pre_s = _c2(x1_s, ada_s, wrt, wsg, wsu, wsd, bs, ts, alpha, n_total, n_p, (u2, lt))

    tok_tile = next(c for c in (512, 256, 128) if n_total % c == 0)
    eidx, wsel, rank, cnt = _route(lt, router_bias[0][:, None], tok_tile)
    counts = cnt[:, 0].astype(I32)
    padded = (counts + MOE_TILE - 1) // MOE_TILE * MOE_TILE
    pend = jnp.cumsum(padded)
    pstart = pend - padded
    dest = pstart[eidx] + rank
    rows_max = (n_total * TOP_K + ne * (MOE_TILE - 1)) // MOE_TILE * MOE_TILE
    n_tiles = rows_max // MOE_TILE
    tile_start = jnp.arange(n_tiles, dtype=I32) * MOE_TILE
    tile_e = jnp.minimum(jnp.searchsorted(pend, tile_start, side="right"), ne - 1).astype(I32)
    tile_nv = jnp.clip(counts[tile_e] - (tile_start - pstart[tile_e]), 0, MOE_TILE).astype(I32)

    xs = _dispatch(dest, u2, rows_max, tok_tile)
    ys = _experts(tile_e, tile_nv, xs, w_exp_gate[0], w_exp_up[0], w_exp_down[0])

    wt = wsel.T
    y_p = _final(dest, pre_p, ada_p, wt, row2(ln2_w), row2(ln2_b), ys, 1, _tile_t(tp, 128), 0)
    y_s = _final(dest, pre_s, ada_s, wt, row2(ln2_w), row2(ln2_b), ys, min(bs, max(1, 128 // ts)), ts, n_p)

    lead = lambda a: a[None]
    return (y_p, y_s) + tuple(lead(a) for a in st_p) + tuple(lead(a) for a in st_s)
```

```python
import functools

import jax
import jax.numpy as jnp
from jax import lax
from jax.experimental import pallas as pl
from jax.experimental.pallas import tpu as pltpu

F32, BF16, I32 = jnp.float32, jnp.bfloat16, jnp.int32
HIGHEST = lax.Precision.HIGHEST
NT_DIMS = (((1,), (1,)), ((), ()))
TN_DIMS = (((0,), (0,)), ((), ()))

CHUNK = 64
SUB = 16
HG_HEADS, HG_DK, HG_DV = 8, 128, 128
ML_HEADS, ML_DK, ML_DV = 4, 128, 256
CONV_W = 4
TOP_K, N_GROUPS, TOPK_GROUPS = 8, 8, 4
ROUTED_SCALE = 2.5
LN_EPS = 1e-5
HG_WIDTH = HG_HEADS * HG_DK
ML_QK_WIDTH = 2 * ML_HEADS * ML_DK
ML_V_WIDTH = ML_HEADS * ML_DV
COL = 1024
GATE_COLS = 256
EXP_CAP = 80.0
MOE_TILE = 256
MIB = 1024 * 1024
VMEM_LIMIT = 56 * MIB


def _cp(n_axes, vmem=VMEM_LIMIT):
    return pltpu.CompilerParams(dimension_semantics=("arbitrary",) * n_axes, vmem_limit_bytes=vmem)


def _sds(shape, dtype):
    return jax.ShapeDtypeStruct(shape, dtype)


def _const_spec(shape):
    n = len(shape)
    return pl.BlockSpec(shape, lambda *_: (0,) * n, pipeline_mode=pl.Buffered(1))


def _sigmoid(x):
    return jax.nn.sigmoid(x)


def _silu(x):
    return x * jax.nn.sigmoid(x)


def _ln(x):
    mu = jnp.mean(x, axis=-1, keepdims=True)
    xc = x - mu
    var = jnp.mean(xc * xc, axis=-1, keepdims=True)
    return xc * lax.rsqrt(var + LN_EPS)


def _dot(a, b, **kw):
    return jnp.dot(a, b, preferred_element_type=F32, **kw)


def _dot_nt(a, b, **kw):
    return lax.dot_general(a, b, NT_DIMS, preferred_element_type=F32, **kw)


def _dot_tn(a, b, **kw):
    return lax.dot_general(a, b, TN_DIMS, preferred_element_type=F32, **kw)


def _ada_kernel(c_ref, w_ref, b_ref, o_ref):
    a = _silu(c_ref[...]).astype(BF16)
    o_ref[...] = _dot(a, w_ref[...].astype(BF16)) + b_ref[...]


def _ada(c_all, w_ada, b_ada):
    r, d = c_all.shape
    n6 = w_ada.shape[1]
    tn = 1024
    return pl.pallas_call(
        _ada_kernel,
        out_shape=_sds((r, n6), F32),
        grid=(n6 // tn,),
        in_specs=[pl.BlockSpec((r, d), lambda j: (0, 0)),
                  pl.BlockSpec((d, tn), lambda j: (0, j)),
                  pl.BlockSpec((1, tn), lambda j: (0, j))],
        out_specs=pl.BlockSpec((r, tn), lambda j: (0, j)),
        compiler_params=_cp(1),
        name="ada",
    )(c_all, w_ada, b_ada)


def _stage_a_kernel(x_ref, sc_ref, sh_ref, w_ref, b_ref, wg_ref, bg_ref, z16_ref, z32_ref, g_ref, u_scr, *, n16):
    j = pl.program_id(1)

    @pl.when(j == 0)
    def _():
        u = _ln(x_ref[...]) * (1.0 + sc_ref[...]) + sh_ref[...]
        u2d = u.reshape(u_scr.shape).astype(BF16)
        u_scr[...] = u2d
        g_ref[...] = _dot(u2d, wg_ref[...]) + bg_ref[...]

    @pl.when(j < n16)
    def _():
        z16_ref[...] = (_dot(u_scr[...], w_ref[...]) + b_ref[...]).astype(z16_ref.dtype)

    @pl.when(j >= n16)
    def _():
        z32_ref[...] = _dot(u_scr[...], w_ref[...]) + b_ref[...]


def _stage_a(x3, ada3, w_main, b_main, w_gate, b_gate, bb, tt):
    b, t, d = x3.shape
    nt = t // tt
    tm = bb * tt
    ncol = w_main.shape[1] // COL
    n32 = len(Z32_BLOCKS)
    n16 = ncol - n32
    n = b * t
    return pl.pallas_call(
        functools.partial(_stage_a_kernel, n16=n16),
        out_shape=(_sds((n, n16 * COL), BF16), _sds((n, n32 * COL), F32), _sds((n, GATE_COLS), F32)),
        grid=((b // bb) * nt, ncol),
        in_specs=[pl.BlockSpec((bb, tt, d), lambda i, j: (i // nt, i % nt, 0)),
                  pl.BlockSpec((bb, 1, d), lambda i, j: (i // nt, 0, 1)),
                  pl.BlockSpec((bb, 1, d), lambda i, j: (i // nt, 0, 0)),
                  pl.BlockSpec((d, COL), lambda i, j: (0, j)),
                  pl.BlockSpec((1, COL), lambda i, j: (0, j)),
                  _const_spec((d, GATE_COLS)),
                  _const_spec((1, GATE_COLS))],
        out_specs=(pl.BlockSpec((tm, COL), lambda i, j: (i, jnp.minimum(j, n16 - 1))),
                   pl.BlockSpec((tm, COL), lambda i, j: (i, jnp.maximum(j - n16, 0))),
                   pl.BlockSpec((tm, GATE_COLS), lambda i, j: (i, 0))),
        scratch_shapes=[pltpu.VMEM((tm, d), BF16)],
        compiler_params=_cp(2),
        name="stage_a",
    )(x3, ada3, ada3, w_main, b_main, w_gate, b_gate)


Z16_BLOCKS = ("hq", "hi", "hg", "mv", "mo")
Z32_BLOCKS = ("hf", "mqk")


def _zspec(tt, nt, d_model, name):
    if name in Z32_BLOCKS:
        k = Z32_BLOCKS.index(name)
    else:
        k = 2 * d_model // COL + Z16_BLOCKS.index(name)
    return pl.BlockSpec((tt, COL), lambda bi, j: (bi * nt + j, k))


def _hgrn_kernel(hq_ref, hf_ref, hi_ref, hg_ref, lbl_ref, wn_ref, s0_ref, o_ref, sout_ref, st_scr, *, L, nc):
    j = pl.program_id(1)
    heads = st_scr.shape[0]

    @pl.when(j == 0)
    def _():
        for h in range(heads):
            st_scr[h] = s0_ref[0, h].T

    lbl = lbl_ref[...]
    e = jnp.exp(lbl - jnp.max(lbl, axis=0, keepdims=True))
    lb_all = e[0:1] / jnp.sum(e, axis=0, keepdims=True)
    row = lax.broadcasted_iota(I32, (L, L), 0)
    col = lax.broadcasted_iota(I32, (L, L), 1)
    causal = row >= col
    tri = causal.astype(F32)
    wn = wn_ref[...]

    def chunk(c, carry):
        r0 = pl.multiple_of(c * L, L)
        rows = pl.ds(r0, L)
        sig = _sigmoid(hf_ref[rows, :])
        logf = jnp.log(lb_all + (1.0 - lb_all) * sig)
        key = (1.0 - lb_all) * (1.0 - sig)
        q = _silu(hq_ref[rows, :].astype(F32))
        g = _dot(tri, logf, precision=HIGHEST)
        g_last = g[L - 1:L, :]
        qdec = (q * jnp.exp(g)).astype(BF16)
        khat = (key * jnp.exp(g_last - g)).astype(BF16)
        decay = jnp.exp(g_last)
        vb = hi_ref[rows, :].astype(BF16)
        qsub, ksub = [], []
        for i in range(L // SUB):
            sub = slice(i * SUB, (i + 1) * SUB)
            gref = g[i * SUB + SUB // 2:i * SUB + SUB // 2 + 1, :]
            qsub.append((q[sub] * jnp.exp(g[sub] - gref)).astype(BF16))
            ksub.append((key * jnp.exp(jnp.minimum(gref - g, EXP_CAP))).astype(BF16))
        o_inter, att, upd = [], [], []
        for h in range(heads):
            hs = slice(h * HG_DK, (h + 1) * HG_DK)
            o_inter.append(_dot_nt(qdec[:, hs], st_scr[h].astype(BF16)))
            blocks = [_dot_nt(qs[:, hs], ks[:, hs]) for qs, ks in zip(qsub, ksub)]
            att.append(blocks[0] if len(blocks) == 1 else jnp.concatenate(blocks, axis=0))
            upd.append(_dot_tn(vb[:, hs], khat[:, hs]))
        for h in range(heads):
            hs = slice(h * HG_DK, (h + 1) * HG_DK)
            o = o_inter[h] + _dot(jnp.where(causal, att[h], 0.0).astype(BF16), vb[:, hs])
            st_scr[h] = st_scr[h] * decay[:, hs] + upd[h]
            o = o * lax.rsqrt(jnp.mean(o * o, axis=-1, keepdims=True) + LN_EPS) * wn
            o_ref[rows, hs] = (o * _silu(hg_ref[rows, hs].astype(F32))).astype(o_ref.dtype)
        return carry

    lax.fori_loop(0, nc, chunk, 0)

    @pl.when(j == pl.num_programs(1) - 1)
    def _():
        for h in range(heads):
            sout_ref[0, h] = st_scr[h].T


def _hgrn(z16, z32, lb_logits, w_norm, s0, b, t, tt, d_model):
    L = min(CHUNK, t)
    nt = t // tt
    zspec = functools.partial(_zspec, tt, nt, d_model)
    return pl.pallas_call(
        functools.partial(_hgrn_kernel, L=L, nc=tt // L),
        out_shape=(_sds((b * t, HG_WIDTH), BF16), _sds(s0.shape, F32)),
        grid=(b, nt),
        in_specs=[zspec("hq"), zspec("hf"), zspec("hi"), zspec("hg"),
                  _const_spec(lb_logits.shape), _const_spec(w_norm.shape),
                  pl.BlockSpec((1,) + s0.shape[1:], lambda bi, j: (bi, 0, 0, 0))],
        out_specs=(pl.BlockSpec((tt, HG_WIDTH), lambda bi, j: (bi * nt + j, 0)),
                   pl.BlockSpec((1,) + s0.shape[1:], lambda bi, j: (bi, 0, 0, 0))),
        scratch_shapes=[pltpu.VMEM((HG_HEADS, HG_DV, HG_DK), F32)],
        compiler_params=_cp(2),
        name="hgrn",
    )(z16, z32, z16, z16, lb_logits, w_norm, s0)


def _mlstm_kernel(qk_ref, v_ref, mo_ref, g_ref, conv0_ref, wc_ref, bc_ref, wn_ref, c0_ref, n0_ref, m0_ref,
                  o_ref, cout_ref, nout_ref, mout_ref, convout_ref,
                  xbuf, qk_scr, c_scr, n_scr, m_scr, *, L, nc):
    j = pl.program_id(1)
    last = pl.num_programs(1) - 1
    heads = c_scr.shape[0]
    tt = qk_ref.shape[0]
    P = CONV_W - 1
    OFF = 8
    QW = heads * ML_DK

    @pl.when(j == 0)
    def _():
        xbuf[OFF - P:OFF, :] = conv0_ref[0]
        c_scr[...] = c0_ref[0]
        n_scr[0:heads, :] = n0_ref[0]
        m0 = m0_ref[0]
        for h in range(heads):
            m_scr[h:h + 1, :] = jnp.broadcast_to(m0[:, h:h + 1], (1, 128))

    xbuf[OFF:OFF + tt, :] = qk_ref[...]
    acc = bc_ref[...] + wc_ref[0:1, :] * xbuf[OFF - P:OFF - P + tt, :]
    for tap in range(1, CONV_W):
        acc = acc + wc_ref[tap:tap + 1, :] * xbuf[OFF - P + tap:OFF - P + tap + tt, :]
    qk = _silu(acc)
    qk_scr[:, 0:QW] = qk[:, 0:QW]
    qk_scr[:, QW:2 * QW] = qk[:, QW:2 * QW] * (ML_DK ** -0.5)

    @pl.when(j == last)
    def _():
        convout_ref[0] = xbuf[OFF + tt - P:OFF + tt, :]

    xbuf[OFF - P:OFF, :] = xbuf[OFF + tt - P:OFF + tt, :]

    row = lax.broadcasted_iota(I32, (L, L), 0)
    col = lax.broadcasted_iota(I32, (L, L), 1)
    causal = row >= col
    tri = causal.astype(F32)
    wn = wn_ref[...]

    def chunk(c, carry):
        r0 = pl.multiple_of(c * L, L)
        rows = pl.ds(r0, L)
        gates = g_ref[rows, :]
        gi = gates[:, 0:128]
        gf = gates[:, 128:256]
        lf = jnp.minimum(gf, 0.0) - jnp.log(1.0 + jnp.exp(-jnp.abs(gf)))
        bcum = _dot(tri, lf, precision=HIGHEST)
        a = gi - bcum
        a_t = a.T
        for h in range(heads):
            q = qk_scr[rows, h * ML_DK:(h + 1) * ML_DK]
            k = qk_scr[rows, QW + h * ML_DK:QW + (h + 1) * ML_DK]
            vb = v_ref[rows, h * ML_DV:(h + 1) * ML_DV].astype(BF16)
            b_col = bcum[:, h:h + 1]
            a_col = a[:, h:h + 1]
            a_row = a_t[h:h + 1, :]
            m_prev = m_scr[h:h + 1, 0:1]
            dlog = jnp.where(causal, b_col + a_row, -jnp.inf)
            inter_log = b_col + m_prev
            m_t = jnp.maximum(inter_log, jnp.max(dlog, axis=-1, keepdims=True))
            w = jnp.exp(dlog - m_t)
            inter_w = jnp.exp(inter_log - m_t)
            qb, kb = q.astype(BF16), k.astype(BF16)
            sc = _dot_nt(qb, kb) * w
            cst = c_scr[h]
            num = inter_w * _dot_nt(qb, cst.astype(BF16)) + _dot(sc.astype(BF16), vb)
            nrow = n_scr[h:h + 1, :]
            den = inter_w * jnp.sum(q * nrow, axis=-1, keepdims=True) + jnp.sum(sc, axis=-1, keepdims=True)
            hh = num / jnp.maximum(jnp.abs(den), jnp.exp(-m_t))
            b_last = b_col[L - 1:L, :]
            m_new = m_t[L - 1:L, :]
            decay = jnp.exp(b_last + m_prev - m_new)
            kw = k * jnp.exp(b_last + a_col - m_new)
            c_scr[h] = decay * cst + _dot_tn(vb, kw.astype(BF16))
            n_scr[h:h + 1, :] = decay * nrow + jnp.sum(kw, axis=0, keepdims=True)
            m_scr[h:h + 1, :] = jnp.broadcast_to(m_new, (1, 128))
            mu = jnp.mean(hh, axis=-1, keepdims=True)
            hc = hh - mu
            hn = hc * lax.rsqrt(jnp.mean(hc * hc, axis=-1, keepdims=True) + LN_EPS)
            hn = hn * wn[:, h * ML_DV:(h + 1) * ML_DV]
            og = _sigmoid(mo_ref[rows, h * ML_DV:(h + 1) * ML_DV].astype(F32))
            o_ref[rows, h * ML_DV:(h + 1) * ML_DV] = (hn * og).astype(o_ref.dtype)
        return carry

    lax.fori_loop(0, nc, chunk, 0)

    @pl.when(j == last)
    def _():
        cout_ref[0] = c_scr[...]
        nout_ref[0] = n_scr[0:heads, :]
        lane = lax.broadcasted_iota(I32, (1, 128), 1)
        mrow = jnp.zeros((1, 128), F32)
        for h in range(heads):
            mrow = jnp.where(lane == h, m_scr[h:h + 1, :], mrow)
        mout_ref[0] = mrow[:, 0:heads]


def _mlstm(z16, z32, gates, conv0, w_conv, b_conv, w_norm, c0, n0, m0, b, t, tt, d_model):
    L = min(CHUNK, t)
    nt = t // tt
    m0 = m0.reshape(b, 1, ML_HEADS)
    zspec = functools.partial(_zspec, tt, nt, d_model)
    bspec = lambda a: pl.BlockSpec((1,) + a.shape[1:], lambda bi, j: (bi,) + (0,) * (a.ndim - 1))
    outs = pl.pallas_call(
        functools.partial(_mlstm_kernel, L=L, nc=tt // L),
        out_shape=(_sds((b * t, ML_V_WIDTH), BF16), _sds(c0.shape, F32), _sds(n0.shape, F32),
                   _sds(m0.shape, F32), _sds(conv0.shape, F32)),
        grid=(b, nt),
        in_specs=[zspec("mqk"), zspec("mv"), zspec("mo"),
                  pl.BlockSpec((tt, GATE_COLS), lambda bi, j: (bi * nt + j, 0)),
                  bspec(conv0), _const_spec(w_conv.shape), _const_spec(b_conv.shape), _const_spec(w_norm.shape),
                  bspec(c0), bspec(n0), bspec(m0)],
        out_specs=(pl.BlockSpec((tt, ML_V_WIDTH), lambda bi, j: (bi * nt + j, 0)),
                   bspec(c0), bspec(n0), bspec(m0), bspec(conv0)),
        scratch_shapes=[pltpu.VMEM((tt + 8, ML_QK_WIDTH), F32), pltpu.VMEM((tt, ML_QK_WIDTH), F32),
                        pltpu.VMEM((ML_HEADS, ML_DV, ML_DK), F32), pltpu.VMEM((8, 128), F32),
                        pltpu.VMEM((8, 128), F32)],
        compiler_params=_cp(2),
        name="mlstm",
    )(z32, z16, z16, gates, conv0, w_conv, b_conv, w_norm, c0, n0, m0)
    h_m, c_new, n_new, m_new, conv_new = outs
    return h_m, c_new, n_new, m_new.reshape(b, ML_HEADS), conv_new


def _c1_kernel(x_ref, oh_ref, hm_ref, ga_ref, gb_ref, g1_ref, wpa_ref, wpb_ref, wo_ref, l1w_ref, l1b_ref,
               x1_ref, *, alpha):
    a = _dot(oh_ref[...], wpa_ref[...])
    b = _dot(hm_ref[...], wpb_ref[...])
    merged = _sigmoid(ga_ref[...].astype(F32)) * a + _sigmoid(gb_ref[...].astype(F32)) * b
    y = _dot(merged.astype(BF16), wo_ref[...])
    x = x_ref[...]
    t = alpha * x + (1.0 + g1_ref[...]) * y.reshape(x.shape)
    x1_ref[...] = _ln(t) * l1w_ref[...] + l1b_ref[...]


def _c1(x3, o_h, h_m, z, ada3, wpa, wpb, wo, l1w, l1b, bb, tt, alpha):
    b, t, d = x3.shape
    nt = t // tt
    tm = bb * tt
    xspec = pl.BlockSpec((bb, tt, d), lambda i: (i // nt, i % nt, 0))
    return pl.pallas_call(
        functools.partial(_c1_kernel, alpha=alpha),
        out_shape=_sds(x3.shape, F32),
        grid=((b // bb) * nt,),
        in_specs=[xspec,
                  pl.BlockSpec((tm, HG_WIDTH), lambda i: (i, 0)),
                  pl.BlockSpec((tm, ML_V_WIDTH), lambda i: (i, 0)),
                  pl.BlockSpec((tm, d), lambda i: (i, 0)),
                  pl.BlockSpec((tm, d), lambda i: (i, 1)),
                  pl.BlockSpec((bb, 1, d), lambda i: (i // nt, 0, 2)),
                  _const_spec(wpa.shape), _const_spec(wpb.shape), _const_spec(wo.shape),
                  _const_spec(l1w.shape), _const_spec(l1b.shape)],
        out_specs=xspec,
        compiler_params=_cp(1),
        name="c1",
    )(x3, o_h, h_m, z, z, ada3, wpa, wpb, wo, l1w, l1b)


def _slab(s, rows, nslab):
    return pl.ds(s, rows, stride=nslab)


def _store_slabs(x, out_ref):
    rows, d = x.shape
    nslab = d // 128
    for s in range(nslab):
        out_ref[_slab(s, rows, nslab), :] = x[:, s * 128:(s + 1) * 128]


def _c2_kernel(x1_ref, sh2_ref, sc2_ref, g2_ref, wrt_ref, wsg_ref, wsu_ref, wsd_ref, *rest, alpha):
    u2_ref, lt_ref, pre_ref = rest[-3:]
    x1 = x1_ref[...]
    u = _ln(x1) * (1.0 + sc2_ref[...]) + sh2_ref[...]
    u2d = u.reshape(lt_ref.shape[1], x1.shape[-1])
    _store_slabs(u2d, u2_ref)
    lt_ref[...] = _dot_nt(wrt_ref[...], u2d, precision=HIGHEST)
    ub = u2d.astype(BF16)
    hg = _dot(ub, wsg_ref[...])
    hu = _dot(ub, wsu_ref[...])
    shared = _dot((_silu(hg) * hu).astype(BF16), wsd_ref[...])
    pre_ref[...] = alpha * x1 + (1.0 + g2_ref[...]) * shared.reshape(x1.shape)


def _c2(x1, ada3, wrt, wsg, wsu, wsd, bb, tt, alpha, n_total, row0, shared_bufs):
    b, t, d = x1.shape
    nt = t // tt
    tm = bb * tt
    ne = wrt.shape[0]
    off = row0 // tm
    xspec = pl.BlockSpec((bb, tt, d), lambda i: (i // nt, i % nt, 0))
    mod = lambda c: pl.BlockSpec((bb, 1, d), lambda i: (i // nt, 0, c))
    in_specs = [xspec, mod(3), mod(4), mod(5),
                _const_spec(wrt.shape), _const_spec(wsg.shape), _const_spec(wsu.shape), _const_spec(wsd.shape)]
    args = [x1, ada3, ada3, ada3, wrt, wsg, wsu, wsd]
    aliases = {}
    if shared_bufs is not None:
        in_specs += [pl.BlockSpec(memory_space=pl.ANY), pl.BlockSpec(memory_space=pl.ANY)]
        aliases = {len(args): 0, len(args) + 1: 1}
        args += list(shared_bufs)
    return pl.pallas_call(
        functools.partial(_c2_kernel, alpha=alpha),
        out_shape=(_sds((n_total * (d // 128), 128), F32), _sds((ne, n_total), F32), _sds(x1.shape, F32)),
        grid=((b // bb) * nt,),
        in_specs=in_specs,
        out_specs=(pl.BlockSpec((tm * (d // 128), 128), lambda i: (off + i, 0)),
                   pl.BlockSpec((ne, tm), lambda i: (0, off + i)),
                   xspec),
        input_output_aliases=aliases,
        compiler_params=_cp(1),
        name="c2",
    )(*args)


def _route_kernel(lt_ref, bias_ref, eidx_ref, w_ref, rank_ref, cnt_ref, carry_scr):
    @pl.when(pl.program_id(0) == 0)
    def _():
        carry_scr[...] = jnp.zeros_like(carry_scr)

    ne, tr = lt_ref.shape
    gsz = ne // N_GROUPS
    s = _sigmoid(lt_ref[...])
    biased = s + bias_ref[...]
    neg = -jnp.inf
    blocks, gscore = [], []
    gio = lax.broadcasted_iota(I32, (gsz, tr), 0)
    for g in range(N_GROUPS):
        blk = biased[g * gsz:(g + 1) * gsz, :]
        m1 = jnp.max(blk, axis=0, keepdims=True)
        i1 = jnp.min(jnp.where(blk == m1, gio, gsz), axis=0, keepdims=True)
        m2 = jnp.max(jnp.where(gio == i1, neg, blk), axis=0, keepdims=True)
        blocks.append(blk)
        gscore.append(m1 + m2)
    masked = []
    for g in range(N_GROUPS):
        beat = jnp.zeros((1, tr), I32)
        for o in range(N_GROUPS):
            if o == g:
                continue
            wins = (gscore[o] >= gscore[g]) if o < g else (gscore[o] > gscore[g])
            beat = beat + wins.astype(I32)
        masked.append(jnp.where(beat < TOPK_GROUPS, blocks[g], neg))
    masked = jnp.concatenate(masked, axis=0)
    eio = lax.broadcasted_iota(I32, (ne, tr), 0)
    multihot = jnp.zeros((ne, tr), F32)
    idxs, ws = [], []
    for _ in range(TOP_K):
        mx = jnp.max(masked, axis=0, keepdims=True)
        idx = jnp.min(jnp.where(masked == mx, eio, ne), axis=0, keepdims=True)
        hit = eio == idx
        ws.append(jnp.sum(jnp.where(hit, s, 0.0), axis=0, keepdims=True))
        idxs.append(idx)
        masked = jnp.where(hit, neg, masked)
        multihot = multihot + hit.astype(F32)
    wsum = ws[0]
    for k in range(1, TOP_K):
        wsum = wsum + ws[k]
    tr_r = lax.broadcasted_iota(I32, (tr, tr), 0)
    tr_c = lax.broadcasted_iota(I32, (tr, tr), 1)
    before = (tr_r < tr_c).astype(BF16)
    ranks = _dot(multihot.astype(BF16), before) + carry_scr[:, 0:1]
    for k in range(TOP_K):
        eidx_ref[k:k + 1, :] = idxs[k]
        w_ref[k:k + 1, :] = ws[k] / wsum * ROUTED_SCALE
        rank_ref[k:k + 1, :] = jnp.sum(jnp.where(eio == idxs[k], ranks, 0.0), axis=0, keepdims=True).astype(I32)
    carry_scr[...] = carry_scr[...] + jnp.sum(multihot, axis=1, keepdims=True)
    cnt_ref[...] = carry_scr[...]


def _route(lt, bias_col, tr):
    ne, n = lt.shape
    kspec = pl.BlockSpec((TOP_K, tr), lambda i: (0, i))
    return pl.pallas_call(
        _route_kernel,
        out_shape=(_sds((TOP_K, n), I32), _sds((TOP_K, n), F32), _sds((TOP_K, n), I32), _sds((ne, 128), F32)),
        grid=(n // tr,),
        in_specs=[pl.BlockSpec((ne, tr), lambda i: (0, i)), _const_spec(bias_col.shape)],
        out_specs=(kspec, kspec, kspec, pl.BlockSpec((ne, 128), lambda i: (0, 0))),
        scratch_shapes=[pltpu.VMEM((ne, 128), F32)],
        compiler_params=_cp(1),
        name="route",
    )(lt, bias_col)


def _dest_kernel(eidx_ref, rank_ref, pstart_ref, dest_ref, *, nslab):
    ne = pstart_ref.shape[0]
    tr = eidx_ref.shape[1]
    eio = lax.broadcasted_iota(I32, (ne, tr), 0)
    pstart = pstart_ref[...]
    for k in range(TOP_K):
        start = jnp.sum(jnp.where(eio == eidx_ref[k:k + 1, :], pstart, 0.0), axis=0, keepdims=True)
        dest_ref[k:k + 1, :] = (start.astype(I32) + rank_ref[k:k + 1, :]) * nslab


def _dest(eidx, rank, pstart_col, tr, nslab):
    n = eidx.shape[1]
    kspec = pl.BlockSpec((TOP_K, tr), lambda i: (0, i))
    return pl.pallas_call(
        functools.partial(_dest_kernel, nslab=nslab),
        out_shape=_sds((TOP_K, n), I32),
        grid=(n // tr,),
        in_specs=[kspec, kspec, _const_spec(pstart_col.shape)],
        out_specs=kspec,
        compiler_params=_cp(1),
        name="dest",
    )(eidx, rank, pstart_col)


def _dispatch_kernel(dest_ref, u2_ref, xs_hbm, sem, *, td, nslab):
    def issue(t, carry):
        src = u2_ref.at[pl.ds(pl.multiple_of(t * nslab, nslab), nslab)]
        for k in range(TOP_K):
            row = pl.multiple_of(dest_ref[t * TOP_K + k], nslab)
            pltpu.make_async_copy(src, xs_hbm.at[pl.ds(row, nslab)], sem).start()
        return carry

    lax.fori_loop(0, td, issue, 0, unroll=2)
    for _ in range(TOP_K):
        pltpu.make_async_copy(u2_ref, xs_hbm.at[pl.ds(0, td * nslab)], sem).wait()


def _dispatch(dest_flat, u2p, rows_max, td, nslab):
    lanes = u2p.shape[1]
    n = u2p.shape[0] // nslab
    return pl.pallas_call(
        functools.partial(_dispatch_kernel, td=td, nslab=nslab),
        out_shape=_sds((rows_max * nslab, lanes), u2p.dtype),
        grid=(n // td,),
        in_specs=[pl.BlockSpec((td * TOP_K,), lambda i: (i,), memory_space=pltpu.SMEM),
                  pl.BlockSpec((td * nslab, lanes), lambda i: (i, 0))],
        out_specs=pl.BlockSpec(memory_space=pl.ANY),
        scratch_shapes=[pltpu.SemaphoreType.DMA],
        compiler_params=_cp(1),
        name="dispatch",
    )(dest_flat, u2p)


def _expert_kernel(te_ref, nv_ref, nxt_ref, slot_ref, xs_ref, wg_hbm, wu_hbm, wd_hbm, y_ref,
                   stg_g, stg_u, stg_d, wgb, wub, wdb, xb, sems):
    j = pl.program_id(0)
    n = pl.num_programs(0)
    ne = wg_hbm.shape[0]
    e = te_ref[j]
    nv = nv_ref[j]
    slot = slot_ref[j]
    used = nv > 0
    first = (j == 0) | (te_ref[jnp.maximum(j - 1, 0)] != e)
    jn = jnp.minimum(j + 1, n - 1)
    last = (j == n - 1) | (te_ref[jn] != e) | (nv_ref[jn] == 0)
    has_next = nxt_ref[j] < ne
    nxt = jnp.minimum(nxt_ref[j], ne - 1)

    def weight_copies(expert):
        return (pltpu.make_async_copy(wg_hbm.at[expert], stg_g, sems.at[0]),
                pltpu.make_async_copy(wu_hbm.at[expert], stg_u, sems.at[1]),
                pltpu.make_async_copy(wd_hbm.at[expert], stg_d, sems.at[2]))

    def land(expert, dst_slot):
        for c in weight_copies(expert):
            c.wait()
        wgb[dst_slot] = stg_g[...].astype(BF16)
        wub[dst_slot] = stg_u[...].astype(BF16)
        wdb[dst_slot] = stg_d[...].astype(BF16)

    @pl.when(used & (j == 0))
    def _():
        for c in weight_copies(e):
            c.start()
        land(e, slot)

    @pl.when(used & first & has_next)
    def _():
        for c in weight_copies(nxt):
            c.start()

    @pl.when(used)
    def _():
        tm, d = xb.shape
        nslab = d // 128
        valid = lax.broadcasted_iota(I32, (tm, 1), 0) < nv
        for s in range(nslab):
            xb[:, s * 128:(s + 1) * 128] = jnp.where(valid, xs_ref[_slab(s, tm, nslab), :], 0.0).astype(BF16)
        x = xb[...]
        hg = _dot(x, wgb[slot])
        hu = _dot(x, wub[slot])
        _store_slabs(_dot((_silu(hg) * hu).astype(BF16), wdb[slot]), y_ref)

    @pl.when(used & last & has_next)
    def _():
        land(nxt, 1 - slot)


def _experts(tile_e, tile_nv, tile_nxt, tile_slot, xs, w_gate, w_up, w_down):
    ne, d, f = w_gate.shape
    nslab = d // 128
    lanes = xs.shape[1]
    rows = xs.shape[0] // nslab
    tm = MOE_TILE
    xspec = pl.BlockSpec((tm * nslab, lanes), lambda j, *_: (j, 0))
    grid_spec = pltpu.PrefetchScalarGridSpec(
        num_scalar_prefetch=4,
        grid=(rows // tm,),
        in_specs=[xspec, pl.BlockSpec(memory_space=pl.ANY), pl.BlockSpec(memory_space=pl.ANY),
                  pl.BlockSpec(memory_space=pl.ANY)],
        out_specs=xspec,
        scratch_shapes=[pltpu.VMEM((d, f), F32), pltpu.VMEM((d, f), F32), pltpu.VMEM((f, d), F32),
                        pltpu.VMEM((2, d, f), BF16), pltpu.VMEM((2, d, f), BF16), pltpu.VMEM((2, f, d), BF16),
                        pltpu.VMEM((tm, d), BF16), pltpu.SemaphoreType.DMA((3,))],
    )
    return pl.pallas_call(
        _expert_kernel,
        out_shape=_sds(xs.shape, F32),
        grid_spec=grid_spec,
        compiler_params=_cp(1),
        name="experts",
    )(tile_e, tile_nv, tile_nxt, tile_slot, xs, w_gate, w_up, w_down)


def _final_kernel(dest_ref, dnext_ref, pre_ref, g2_ref, wt_ref, l2w_ref, l2b_ref, ys_hbm, out_ref,
                  buf, routed, sems):
    i = pl.program_id(0)
    n = pl.num_programs(0)
    tm, d = routed.shape
    nslab = d // 128
    cur = i % 2

    def gather(idx_ref, half):
        def issue(t, carry):
            dst_rows = pl.ds(pl.multiple_of(t * nslab, nslab), nslab)
            for k in range(TOP_K):
                row = pl.multiple_of(idx_ref[t * TOP_K + k], nslab)
                pltpu.make_async_copy(ys_hbm.at[pl.ds(row, nslab)], buf.at[half, k, dst_rows],
                                      sems.at[half]).start()
            return carry

        lax.fori_loop(0, tm, issue, 0, unroll=2)

    @pl.when(i == 0)
    def _():
        gather(dest_ref, cur)

    @pl.when(i + 1 < n)
    def _():
        gather(dnext_ref, 1 - cur)

    for k in range(TOP_K):
        pltpu.make_async_copy(ys_hbm.at[pl.ds(0, tm * nslab)], buf.at[cur, k], sems.at[cur]).wait()
    wt = wt_ref[...]
    for s in range(nslab):
        acc = buf[cur, 0, _slab(s, tm, nslab), :] * wt[:, 0:1]
        for k in range(1, TOP_K):
            acc = acc + buf[cur, k, _slab(s, tm, nslab), :] * wt[:, k:k + 1]
        routed[:, s * 128:(s + 1) * 128] = acc
    pre = pre_ref[...]
    t = pre + (1.0 + g2_ref[...]) * routed[...].reshape(pre.shape)
    out_ref[...] = _ln(t) * l2w_ref[...] + l2b_ref[...]


def _final(dest, pre, ada3, wt, l2w, l2b, ys, bb, tt, row0):
    b, t, d = pre.shape
    nt = t // tt
    tm = bb * tt
    off = row0 // tm
    xspec = pl.BlockSpec((bb, tt, d), lambda i: (i // nt, i % nt, 0))
    n_steps = (b // bb) * nt
    return pl.pallas_call(
        _final_kernel,
        out_shape=_sds(pre.shape, F32),
        grid=(n_steps,),
        in_specs=[pl.BlockSpec((tm * TOP_K,), lambda i: (off + i,), memory_space=pltpu.SMEM),
                  pl.BlockSpec((tm * TOP_K,), lambda i: (off + jnp.minimum(i + 1, n_steps - 1),),
                               memory_space=pltpu.SMEM),
                  xspec,
                  pl.BlockSpec((bb, 1, d), lambda i: (i // nt, 0, 5)),
                  pl.BlockSpec((tm, TOP_K), lambda i: (off + i, 0)),
                  _const_spec(l2w.shape), _const_spec(l2b.shape),
                  pl.BlockSpec(memory_space=pl.ANY)],
        out_specs=xspec,
        scratch_shapes=[pltpu.VMEM((2, TOP_K, tm * (d // 128), ys.shape[1]), F32), pltpu.VMEM((tm, d), F32),
                        pltpu.SemaphoreType.DMA((2,))],
        compiler_params=_cp(1),
        name="final",
    )(dest, dest, pre, ada3, wt, l2w, l2b, ys)


def _tile_t(t, cap):
    return min(t, cap)


def kernel(x_prompt, x_sample, state_hgrn, state_mlstm_c, state_mlstm_n, state_mlstm_m, state_mlstm_conv, c_prompt, c_sample, w_ada, b_ada, w_in, b_in, lb_logits, w_hg_norm, w_conv, b_conv, w_ml_norm, w_proj_a, w_proj_b, w_out, ln1_w, ln1_b, w_router, router_bias, w_exp_gate, w_exp_up, w_exp_down, w_sh_gate, w_sh_up, w_sh_down, ln2_w, ln2_b):
    depth = w_in.shape[0]
    assert depth == 1, "single-layer trunk"
    alpha = (2.0 * depth) ** 0.25
    bp, tp, d = x_prompt.shape
    bs, ts, _ = x_sample.shape
    n_p, n_s = bp * tp, bs * ts
    n_total = n_p + n_s
    ne = w_router.shape[-1]

    w = w_in[0]
    o_hg, o_mqk, o_mv, o_mo = 0, 4 * HG_WIDTH, 4 * HG_WIDTH + ML_QK_WIDTH, 4 * HG_WIDTH + ML_QK_WIDTH + ML_V_WIDTH
    o_mi = o_mo + ML_V_WIDTH
    o_mf = o_mi + ML_HEADS
    o_ga = o_mf + ML_HEADS
    o_gb = o_ga + d
    col_of = {"hq": o_hg, "hf": o_hg + HG_WIDTH, "hi": o_hg + 2 * HG_WIDTH, "hg": o_hg + 3 * HG_WIDTH,
              "mqk": o_mqk, "mv": o_mv, "mo": o_mo}
    regroup = lambda a: jnp.concatenate(
        [a[..., o_ga:o_gb + d]] + [a[..., col_of[k]:col_of[k] + COL] for k in Z16_BLOCKS + Z32_BLOCKS], axis=-1)
    w_main = regroup(w).astype(BF16)
    b_main = regroup(b_in[0])[None, :]
    pad = lambda a: jnp.pad(a, [(0, 0)] * (a.ndim - 1) + [(0, 128 - ML_HEADS)])
    gate_cols = lambda a: jnp.concatenate([pad(a[..., o_mi:o_mf]), pad(a[..., o_mf:o_ga])], axis=-1)
    w_gate = gate_cols(w).astype(BF16)
    b_gate = gate_cols(b_in[0])[None, :]
    wpa, wpb, wo = w_proj_a[0].astype(BF16), w_proj_b[0].astype(BF16), w_out[0].astype(BF16)
    wrt = w_router[0].T
    wsg, wsu, wsd = w_sh_gate[0].astype(BF16), w_sh_up[0].astype(BF16), w_sh_down[0].astype(BF16)
    row2 = lambda a: a[0][None, :]

    n_c = bp + bs
    c_all = jnp.pad(jnp.concatenate([c_prompt, c_sample], axis=0), [(0, -n_c % 8), (0, 0)])
    ada = _ada(c_all, w_ada[0], row2(b_ada))
    ada_p = ada[:bp][:, None, :]
    ada_s = ada[bp:n_c][:, None, :]

    def mixers(x3, ada3, s0, c0, n0, m0, conv0, bb, tt, tt_rec):
        b, t, _ = x3.shape
        z16, z32, gates = _stage_a(x3, ada3, w_main, b_main, w_gate, b_gate, bb, tt)
        o_h, s_new = _hgrn(z16, z32, lb_logits, w_hg_norm, s0, b, t, tt_rec, d)
        h_m, c_new, n_new, m_new, conv_new = _mlstm(z16, z32, gates, conv0, w_conv[0], row2(b_conv),
                                                    row2(w_ml_norm), c0, n0, m0, b, t, tt_rec, d)
        return z16, o_h, h_m, (s_new, c_new, n_new, m_new, conv_new)

    zeros = lambda *s: jnp.zeros(s, F32)
    z_p, oh_p, hm_p, st_p = mixers(x_prompt, ada_p, zeros(bp, HG_HEADS, HG_DK, HG_DV), zeros(bp, ML_HEADS, ML_DV, ML_DK),
                                   zeros(bp, ML_HEADS, ML_DK), zeros(bp, ML_HEADS), zeros(bp, CONV_W - 1, ML_QK_WIDTH),
                                   1, _tile_t(tp, 1024), _tile_t(tp, 512))
    z_s, oh_s, hm_s, st_s = mixers(x_sample, ada_s, state_hgrn[0], state_mlstm_c[0], state_mlstm_n[0],
                                   state_mlstm_m[0], state_mlstm_conv[0], bs, ts, ts)

    tc_p = _tile_t(tp, 256)
    x1_p = _c1(x_prompt, oh_p, hm_p, z_p, ada_p, wpa, wpb, wo, row2(ln1_w), row2(ln1_b), 1, tc_p, alpha)
    x1_s = _c1(x_sample, oh_s, hm_s, z_s, ada_s, wpa, wpb, wo, row2(ln1_w), row2(ln1_b), bs, ts, alpha)
    u2, lt, pre_p = _c2(x1_p, ada_p, wrt, wsg, wsu, wsd, 1, tc_p, alpha, n_total, 0, None)
    u2, lt, pre_s = _c2(x1_s, ada_s, wrt, wsg, wsu, wsd, bs, ts, alpha, n_total, n_p, (u2, lt))

    tok_tile = next(c for c in (512, 256, 128) if n_total % c == 0)
    eidx, wsel, rank, cnt = _route(lt, router_bias[0][:, None], tok_tile)
    counts = cnt[:, 0].astype(I32)
    padded = (counts + MOE_TILE - 1) // MOE_TILE * MOE_TILE
    pend = jnp.cumsum(padded)
    pstart = pend - padded
    nslab = d // 128
    dest = _dest(eidx, rank, pstart.astype(F32)[:, None], tok_tile, nslab)
    dest = dest.T.reshape(-1)
    rows_max = (n_total * TOP_K + ne * (MOE_TILE - 1)) // MOE_TILE * MOE_TILE
    n_tiles = rows_max // MOE_TILE
    tile_start = jnp.arange(n_tiles, dtype=I32) * MOE_TILE
    owner = (tile_start[:, None] >= pstart[None, :]) & (tile_start[:, None] < pend[None, :])
    tile_nv = jnp.sum(jnp.where(owner, jnp.clip(pstart + counts - tile_start[:, None], 0, MOE_TILE), 0), axis=1)
    tile_e = jnp.minimum(jnp.sum((pend[None, :] <= tile_start[:, None]).astype(I32), axis=1), ne - 1)

    eids = jnp.arange(ne, dtype=I32)
    has_rows = counts > 0
    nxt_e = lax.cummin(jnp.where(has_rows, eids, ne), reverse=True)
    nxt_e = jnp.concatenate([nxt_e[1:], jnp.full((1,), ne, I32)])
    slot_e = (jnp.cumsum(has_rows.astype(I32)) - 1) % 2
    pick = lambda tab: jnp.sum(jnp.where(tile_e[:, None] == eids[None, :], tab[None, :], 0), axis=1).astype(I32)

    xs = _dispatch(dest, u2, rows_max, tok_tile, nslab)
    ys = _experts(tile_e, tile_nv.astype(I32), pick(nxt_e), pick(slot_e), xs,
                  w_exp_gate[0], w_exp_up[0], w_exp_down[0])

    wt = wsel.T
    y_p = _final(dest, pre_p, ada_p, wt, row2(ln2_w), row2(ln2_b), ys, 1, _tile_t(tp, 128), 0)
    y_s = _final(dest, pre_s, ada_s, wt, row2(ln2_w), row2(ln2_b), ys, min(bs, max(1, 128 // ts)), ts, n_p)

    lead = lambda a: a[None]
    return (y_p, y_s) + tuple(lead(a) for a in st_p) + tuple(lead(a) for a in st_s)
```

```python
import functools

import jax
import jax.numpy as jnp
from jax import lax
from jax.experimental import pallas as pl
from jax.experimental.pallas import tpu as pltpu

F32, BF16, I32 = jnp.float32, jnp.bfloat16, jnp.int32
HIGHEST = lax.Precision.HIGHEST
NT_DIMS = (((1,), (1,)), ((), ()))
TN_DIMS = (((0,), (0,)), ((), ()))

CHUNK = 64
SUB = 16
HG_HEADS, HG_DK, HG_DV = 8, 128, 128
ML_HEADS, ML_DK, ML_DV = 4, 128, 256
CONV_W = 4
TOP_K, N_GROUPS, TOPK_GROUPS = 8, 8, 4
ROUTED_SCALE = 2.5
LN_EPS = 1e-5
HG_WIDTH = HG_HEADS * HG_DK
ML_QK_WIDTH = 2 * ML_HEADS * ML_DK
ML_V_WIDTH = ML_HEADS * ML_DV
COL = 1024
GATE_COLS = 256
EXP_CAP = 80.0
MOE_TILE = 512
MIB = 1024 * 1024
VMEM_LIMIT = 56 * MIB


def _cp(n_axes, vmem=VMEM_LIMIT):
    return pltpu.CompilerParams(dimension_semantics=("arbitrary",) * n_axes, vmem_limit_bytes=vmem)


def _sds(shape, dtype):
    return jax.ShapeDtypeStruct(shape, dtype)


def _const_spec(shape):
    n = len(shape)
    return pl.BlockSpec(shape, lambda *_: (0,) * n, pipeline_mode=pl.Buffered(1))


def _sigmoid(x):
    return jax.nn.sigmoid(x)


def _silu(x):
    return x * jax.nn.sigmoid(x)


def _ln(x):
    mu = jnp.mean(x, axis=-1, keepdims=True)
    xc = x - mu
    var = jnp.mean(xc * xc, axis=-1, keepdims=True)
    return xc * lax.rsqrt(var + LN_EPS)


def _dot(a, b, **kw):
    return jnp.dot(a, b, preferred_element_type=F32, **kw)


def _dot_nt(a, b, **kw):
    return lax.dot_general(a, b, NT_DIMS, preferred_element_type=F32, **kw)


def _dot_tn(a, b, **kw):
    return lax.dot_general(a, b, TN_DIMS, preferred_element_type=F32, **kw)


def _ada_kernel(c_ref, w_ref, b_ref, o_ref):
    a = _silu(c_ref[...]).astype(BF16)
    o_ref[...] = _dot(a, w_ref[...].astype(BF16)) + b_ref[...]


def _ada(c_all, w_ada, b_ada):
    r, d = c_all.shape
    n6 = w_ada.shape[1]
    tn = 1024
    return pl.pallas_call(
        _ada_kernel,
        out_shape=_sds((r, n6), F32),
        grid=(n6 // tn,),
        in_specs=[pl.BlockSpec((r, d), lambda j: (0, 0)),
                  pl.BlockSpec((d, tn), lambda j: (0, j)),
                  pl.BlockSpec((1, tn), lambda j: (0, j))],
        out_specs=pl.BlockSpec((r, tn), lambda j: (0, j)),
        compiler_params=_cp(1),
        name="ada",
    )(c_all, w_ada, b_ada)


def _stage_a_kernel(x_ref, sc_ref, sh_ref, w_ref, b_ref, wg_ref, bg_ref, z16_ref, z32_ref, g_ref, u_scr, *, n16):
    j = pl.program_id(1)

    @pl.when(j == 0)
    def _():
        u = _ln(x_ref[...]) * (1.0 + sc_ref[...]) + sh_ref[...]
        u2d = u.reshape(u_scr.shape).astype(BF16)
        u_scr[...] = u2d
        g_ref[...] = _dot(u2d, wg_ref[...]) + bg_ref[...]

    @pl.when(j < n16)
    def _():
        z16_ref[...] = (_dot(u_scr[...], w_ref[...]) + b_ref[...]).astype(z16_ref.dtype)

    @pl.when(j >= n16)
    def _():
        z32_ref[...] = _dot(u_scr[...], w_ref[...]) + b_ref[...]


def _stage_a(x3, ada3, w_main, b_main, w_gate, b_gate, bb, tt):
    b, t, d = x3.shape
    nt = t // tt
    tm = bb * tt
    ncol = w_main.shape[1] // COL
    n32 = len(Z32_BLOCKS)
    n16 = ncol - n32
    n = b * t
    return pl.pallas_call(
        functools.partial(_stage_a_kernel, n16=n16),
        out_shape=(_sds((n, n16 * COL), BF16), _sds((n, n32 * COL), F32), _sds((n, GATE_COLS), F32)),
        grid=((b // bb) * nt, ncol),
        in_specs=[pl.BlockSpec((bb, tt, d), lambda i, j: (i // nt, i % nt, 0)),
                  pl.BlockSpec((bb, 1, d), lambda i, j: (i // nt, 0, 1)),
                  pl.BlockSpec((bb, 1, d), lambda i, j: (i // nt, 0, 0)),
                  pl.BlockSpec((d, COL), lambda i, j: (0, j)),
                  pl.BlockSpec((1, COL), lambda i, j: (0, j)),
                  _const_spec((d, GATE_COLS)),
                  _const_spec((1, GATE_COLS))],
        out_specs=(pl.BlockSpec((tm, COL), lambda i, j: (i, jnp.minimum(j, n16 - 1))),
                   pl.BlockSpec((tm, COL), lambda i, j: (i, jnp.maximum(j - n16, 0))),
                   pl.BlockSpec((tm, GATE_COLS), lambda i, j: (i, 0))),
        scratch_shapes=[pltpu.VMEM((tm, d), BF16)],
        compiler_params=_cp(2),
        name="stage_a",
    )(x3, ada3, ada3, w_main, b_main, w_gate, b_gate)


Z16_BLOCKS = ("hq", "hi", "hg", "mv", "mo")
Z32_BLOCKS = ("hf", "mqk")


def _zspec(tt, nt, d_model, name):
    if name in Z32_BLOCKS:
        k = Z32_BLOCKS.index(name)
    else:
        k = 2 * d_model // COL + Z16_BLOCKS.index(name)
    return pl.BlockSpec((tt, COL), lambda bi, j: (bi * nt + j, k))


def _hgrn_kernel(hq_ref, hf_ref, hi_ref, hg_ref, lbl_ref, wn_ref, s0_ref, o_ref, sout_ref, st_scr, *, L, nc):
    j = pl.program_id(1)
    heads = st_scr.shape[0]

    @pl.when(j == 0)
    def _():
        for h in range(heads):
            st_scr[h] = s0_ref[0, h].T

    lbl = lbl_ref[...]
    e = jnp.exp(lbl - jnp.max(lbl, axis=0, keepdims=True))
    lb_all = e[0:1] / jnp.sum(e, axis=0, keepdims=True)
    row = lax.broadcasted_iota(I32, (L, L), 0)
    col = lax.broadcasted_iota(I32, (L, L), 1)
    causal = row >= col
    tri = causal.astype(F32)
    wn = wn_ref[...]

    def chunk(c, carry):
        r0 = pl.multiple_of(c * L, L)
        rows = pl.ds(r0, L)
        sig = _sigmoid(hf_ref[rows, :])
        logf = jnp.log(lb_all + (1.0 - lb_all) * sig)
        key = (1.0 - lb_all) * (1.0 - sig)
        q = _silu(hq_ref[rows, :].astype(F32))
        g = _dot(tri, logf, precision=HIGHEST)
        g_last = g[L - 1:L, :]
        qdec = (q * jnp.exp(g)).astype(BF16)
        khat = (key * jnp.exp(g_last - g)).astype(BF16)
        decay = jnp.exp(g_last)
        vb = hi_ref[rows, :].astype(BF16)
        qsub, ksub = [], []
        for i in range(L // SUB):
            sub = slice(i * SUB, (i + 1) * SUB)
            gref = g[i * SUB + SUB // 2:i * SUB + SUB // 2 + 1, :]
            qsub.append((q[sub] * jnp.exp(g[sub] - gref)).astype(BF16))
            ksub.append((key * jnp.exp(jnp.minimum(gref - g, EXP_CAP))).astype(BF16))
        o_inter, att, upd = [], [], []
        for h in range(heads):
            hs = slice(h * HG_DK, (h + 1) * HG_DK)
            o_inter.append(_dot_nt(qdec[:, hs], st_scr[h].astype(BF16)))
            blocks = [_dot_nt(qs[:, hs], ks[:, hs]) for qs, ks in zip(qsub, ksub)]
            att.append(blocks[0] if len(blocks) == 1 else jnp.concatenate(blocks, axis=0))
            upd.append(_dot_tn(vb[:, hs], khat[:, hs]))
        for h in range(heads):
            hs = slice(h * HG_DK, (h + 1) * HG_DK)
            o = o_inter[h] + _dot(jnp.where(causal, att[h], 0.0).astype(BF16), vb[:, hs])
            st_scr[h] = st_scr[h] * decay[:, hs] + upd[h]
            o = o * lax.rsqrt(jnp.mean(o * o, axis=-1, keepdims=True) + LN_EPS) * wn
            o_ref[rows, hs] = (o * _silu(hg_ref[rows, hs].astype(F32))).astype(o_ref.dtype)
        return carry

    lax.fori_loop(0, nc, chunk, 0)

    @pl.when(j == pl.num_programs(1) - 1)
    def _():
        for h in range(heads):
            sout_ref[0, h] = st_scr[h].T


def _hgrn(z16, z32, lb_logits, w_norm, s0, b, t, tt, d_model):
    L = min(CHUNK, t)
    nt = t // tt
    zspec = functools.partial(_zspec, tt, nt, d_model)
    return pl.pallas_call(
        functools.partial(_hgrn_kernel, L=L, nc=tt // L),
        out_shape=(_sds((b * t, HG_WIDTH), BF16), _sds(s0.shape, F32)),
        grid=(b, nt),
        in_specs=[zspec("hq"), zspec("hf"), zspec("hi"), zspec("hg"),
                  _const_spec(lb_logits.shape), _const_spec(w_norm.shape),
                  pl.BlockSpec((1,) + s0.shape[1:], lambda bi, j: (bi, 0, 0, 0))],
        out_specs=(pl.BlockSpec((tt, HG_WIDTH), lambda bi, j: (bi * nt + j, 0)),
                   pl.BlockSpec((1,) + s0.shape[1:], lambda bi, j: (bi, 0, 0, 0))),
        scratch_shapes=[pltpu.VMEM((HG_HEADS, HG_DV, HG_DK), F32)],
        compiler_params=_cp(2),
        name="hgrn",
    )(z16, z32, z16, z16, lb_logits, w_norm, s0)


def _mlstm_kernel(qk_ref, v_ref, mo_ref, g_ref, conv0_ref, wc_ref, bc_ref, wn_ref, c0_ref, n0_ref, m0_ref,
                  o_ref, cout_ref, nout_ref, mout_ref, convout_ref,
                  xbuf, qk_scr, c_scr, n_scr, m_scr, *, L, nc):
    j = pl.program_id(1)
    last = pl.num_programs(1) - 1
    heads = c_scr.shape[0]
    tt = qk_ref.shape[0]
    P = CONV_W - 1
    OFF = 8
    QW = heads * ML_DK

    @pl.when(j == 0)
    def _():
        xbuf[OFF - P:OFF, :] = conv0_ref[0]
        c_scr[...] = c0_ref[0]
        n_scr[0:heads, :] = n0_ref[0]
        m0 = m0_ref[0]
        for h in range(heads):
            m_scr[h:h + 1, :] = jnp.broadcast_to(m0[:, h:h + 1], (1, 128))

    xbuf[OFF:OFF + tt, :] = qk_ref[...]
    acc = bc_ref[...] + wc_ref[0:1, :] * xbuf[OFF - P:OFF - P + tt, :]
    for tap in range(1, CONV_W):
        acc = acc + wc_ref[tap:tap + 1, :] * xbuf[OFF - P + tap:OFF - P + tap + tt, :]
    qk = _silu(acc)
    qk_scr[:, 0:QW] = qk[:, 0:QW]
    qk_scr[:, QW:2 * QW] = qk[:, QW:2 * QW] * (ML_DK ** -0.5)

    @pl.when(j == last)
    def _():
        convout_ref[0] = xbuf[OFF + tt - P:OFF + tt, :]

    xbuf[OFF - P:OFF, :] = xbuf[OFF + tt - P:OFF + tt, :]

    row = lax.broadcasted_iota(I32, (L, L), 0)
    col = lax.broadcasted_iota(I32, (L, L), 1)
    causal = row >= col
    tri = causal.astype(F32)
    wn = wn_ref[...]

    def chunk(c, carry):
        r0 = pl.multiple_of(c * L, L)
        rows = pl.ds(r0, L)
        gates = g_ref[rows, :]
        gi = gates[:, 0:128]
        gf = gates[:, 128:256]
        lf = jnp.minimum(gf, 0.0) - jnp.log(1.0 + jnp.exp(-jnp.abs(gf)))
        bcum = _dot(tri, lf, precision=HIGHEST)
        a = gi - bcum
        a_t = a.T
        for h in range(heads):
            q = qk_scr[rows, h * ML_DK:(h + 1) * ML_DK]
            k = qk_scr[rows, QW + h * ML_DK:QW + (h + 1) * ML_DK]
            vb = v_ref[rows, h * ML_DV:(h + 1) * ML_DV].astype(BF16)
            b_col = bcum[:, h:h + 1]
            a_col = a[:, h:h + 1]
            a_row = a_t[h:h + 1, :]
            m_prev = m_scr[h:h + 1, 0:1]
            dlog = jnp.where(causal, b_col + a_row, -jnp.inf)
            inter_log = b_col + m_prev
            m_t = jnp.maximum(inter_log, jnp.max(dlog, axis=-1, keepdims=True))
            w = jnp.exp(dlog - m_t)
            inter_w = jnp.exp(inter_log - m_t)
            qb, kb = q.astype(BF16), k.astype(BF16)
            sc = _dot_nt(qb, kb) * w
            cst = c_scr[h]
            num = inter_w * _dot_nt(qb, cst.astype(BF16)) + _dot(sc.astype(BF16), vb)
            nrow = n_scr[h:h + 1, :]
            den = inter_w * jnp.sum(q * nrow, axis=-1, keepdims=True) + jnp.sum(sc, axis=-1, keepdims=True)
            hh = num / jnp.maximum(jnp.abs(den), jnp.exp(-m_t))
            b_last = b_col[L - 1:L, :]
            m_new = m_t[L - 1:L, :]
            decay = jnp.exp(b_last + m_prev - m_new)
            kw = k * jnp.exp(b_last + a_col - m_new)
            c_scr[h] = decay * cst + _dot_tn(vb, kw.astype(BF16))
            n_scr[h:h + 1, :] = decay * nrow + jnp.sum(kw, axis=0, keepdims=True)
            m_scr[h:h + 1, :] = jnp.broadcast_to(m_new, (1, 128))
            mu = jnp.mean(hh, axis=-1, keepdims=True)
            hc = hh - mu
            hn = hc * lax.rsqrt(jnp.mean(hc * hc, axis=-1, keepdims=True) + LN_EPS)
            hn = hn * wn[:, h * ML_DV:(h + 1) * ML_DV]
            og = _sigmoid(mo_ref[rows, h * ML_DV:(h + 1) * ML_DV].astype(F32))
            o_ref[rows, h * ML_DV:(h + 1) * ML_DV] = (hn * og).astype(o_ref.dtype)
        return carry

    lax.fori_loop(0, nc, chunk, 0)

    @pl.when(j == last)
    def _():
        cout_ref[0] = c_scr[...]
        nout_ref[0] = n_scr[0:heads, :]
        lane = lax.broadcasted_iota(I32, (1, 128), 1)
        mrow = jnp.zeros((1, 128), F32)
        for h in range(heads):
            mrow = jnp.where(lane == h, m_scr[h:h + 1, :], mrow)
        mout_ref[0] = mrow[:, 0:heads]


def _mlstm(z16, z32, gates, conv0, w_conv, b_conv, w_norm, c0, n0, m0, b, t, tt, d_model):
    L = min(CHUNK, t)
    nt = t // tt
    m0 = m0.reshape(b, 1, ML_HEADS)
    zspec = functools.partial(_zspec, tt, nt, d_model)
    bspec = lambda a: pl.BlockSpec((1,) + a.shape[1:], lambda bi, j: (bi,) + (0,) * (a.ndim - 1))
    outs = pl.pallas_call(
        functools.partial(_mlstm_kernel, L=L, nc=tt // L),
        out_shape=(_sds((b * t, ML_V_WIDTH), BF16), _sds(c0.shape, F32), _sds(n0.shape, F32),
                   _sds(m0.shape, F32), _sds(conv0.shape, F32)),
        grid=(b, nt),
        in_specs=[zspec("mqk"), zspec("mv"), zspec("mo"),
                  pl.BlockSpec((tt, GATE_COLS), lambda bi, j: (bi * nt + j, 0)),
                  bspec(conv0), _const_spec(w_conv.shape), _const_spec(b_conv.shape), _const_spec(w_norm.shape),
                  bspec(c0), bspec(n0), bspec(m0)],
        out_specs=(pl.BlockSpec((tt, ML_V_WIDTH), lambda bi, j: (bi * nt + j, 0)),
                   bspec(c0), bspec(n0), bspec(m0), bspec(conv0)),
        scratch_shapes=[pltpu.VMEM((tt + 8, ML_QK_WIDTH), F32), pltpu.VMEM((tt, ML_QK_WIDTH), F32),
                        pltpu.VMEM((ML_HEADS, ML_DV, ML_DK), F32), pltpu.VMEM((8, 128), F32),
                        pltpu.VMEM((8, 128), F32)],
        compiler_params=_cp(2),
        name="mlstm",
    )(z32, z16, z16, gates, conv0, w_conv, b_conv, w_norm, c0, n0, m0)
    h_m, c_new, n_new, m_new, conv_new = outs
    return h_m, c_new, n_new, m_new.reshape(b, ML_HEADS), conv_new


def _c1_kernel(x_ref, oh_ref, hm_ref, ga_ref, gb_ref, g1_ref, wpa_ref, wpb_ref, wo_ref, l1w_ref, l1b_ref,
               x1_ref, *, alpha):
    a = _dot(oh_ref[...], wpa_ref[...])
    b = _dot(hm_ref[...], wpb_ref[...])
    merged = _sigmoid(ga_ref[...].astype(F32)) * a + _sigmoid(gb_ref[...].astype(F32)) * b
    y = _dot(merged.astype(BF16), wo_ref[...])
    x = x_ref[...]
    t = alpha * x + (1.0 + g1_ref[...]) * y.reshape(x.shape)
    x1_ref[...] = _ln(t) * l1w_ref[...] + l1b_ref[...]


def _c1(x3, o_h, h_m, z, ada3, wpa, wpb, wo, l1w, l1b, bb, tt, alpha):
    b, t, d = x3.shape
    nt = t // tt
    tm = bb * tt
    xspec = pl.BlockSpec((bb, tt, d), lambda i: (i // nt, i % nt, 0))
    return pl.pallas_call(
        functools.partial(_c1_kernel, alpha=alpha),
        out_shape=_sds(x3.shape, F32),
        grid=((b // bb) * nt,),
        in_specs=[xspec,
                  pl.BlockSpec((tm, HG_WIDTH), lambda i: (i, 0)),
                  pl.BlockSpec((tm, ML_V_WIDTH), lambda i: (i, 0)),
                  pl.BlockSpec((tm, d), lambda i: (i, 0)),
                  pl.BlockSpec((tm, d), lambda i: (i, 1)),
                  pl.BlockSpec((bb, 1, d), lambda i: (i // nt, 0, 2)),
                  _const_spec(wpa.shape), _const_spec(wpb.shape), _const_spec(wo.shape),
                  _const_spec(l1w.shape), _const_spec(l1b.shape)],
        out_specs=xspec,
        compiler_params=_cp(1),
        name="c1",
    )(x3, o_h, h_m, z, z, ada3, wpa, wpb, wo, l1w, l1b)


def _c2_kernel(x1_ref, sh2_ref, sc2_ref, g2_ref, wrt_ref, wsg_ref, wsu_ref, wsd_ref, *rest, alpha):
    u2_ref, lt_ref, pre_ref = rest[-3:]
    x1 = x1_ref[...]
    u = _ln(x1) * (1.0 + sc2_ref[...]) + sh2_ref[...]
    u2d = u.reshape(u2_ref.shape)
    u2_ref[...] = u2d
    lt_ref[...] = _dot_nt(wrt_ref[...], u2d, precision=HIGHEST)
    ub = u2d.astype(BF16)
    hg = _dot(ub, wsg_ref[...])
    hu = _dot(ub, wsu_ref[...])
    shared = _dot((_silu(hg) * hu).astype(BF16), wsd_ref[...])
    pre_ref[...] = alpha * x1 + (1.0 + g2_ref[...]) * shared.reshape(x1.shape)


def _c2(x1, ada3, wrt, wsg, wsu, wsd, bb, tt, alpha, n_total, row0, shared_bufs):
    b, t, d = x1.shape
    nt = t // tt
    tm = bb * tt
    ne = wrt.shape[0]
    off = row0 // tm
    xspec = pl.BlockSpec((bb, tt, d), lambda i: (i // nt, i % nt, 0))
    mod = lambda c: pl.BlockSpec((bb, 1, d), lambda i: (i // nt, 0, c))
    in_specs = [xspec, mod(3), mod(4), mod(5),
                _const_spec(wrt.shape), _const_spec(wsg.shape), _const_spec(wsu.shape), _const_spec(wsd.shape)]
    args = [x1, ada3, ada3, ada3, wrt, wsg, wsu, wsd]
    aliases = {}
    if shared_bufs is not None:
        in_specs += [pl.BlockSpec(memory_space=pl.ANY), pl.BlockSpec(memory_space=pl.ANY)]
        aliases = {len(args): 0, len(args) + 1: 1}
        args += list(shared_bufs)
    return pl.pallas_call(
        functools.partial(_c2_kernel, alpha=alpha),
        out_shape=(_sds((n_total, d), F32), _sds((ne, n_total), F32), _sds(x1.shape, F32)),
        grid=((b // bb) * nt,),
        in_specs=in_specs,
        out_specs=(pl.BlockSpec((tm, d), lambda i: (off + i, 0)),
                   pl.BlockSpec((ne, tm), lambda i: (0, off + i)),
                   xspec),
        input_output_aliases=aliases,
        compiler_params=_cp(1),
        name="c2",
    )(*args)


def _route_kernel(lt_ref, bias_ref, eidx_ref, w_ref, rank_ref, cnt_ref, carry_scr):
    @pl.when(pl.program_id(0) == 0)
    def _():
        carry_scr[...] = jnp.zeros_like(carry_scr)

    ne, tr = lt_ref.shape
    gsz = ne // N_GROUPS
    s = _sigmoid(lt_ref[...])
    biased = s + bias_ref[...]
    neg = -jnp.inf
    blocks, gscore = [], []
    gio = lax.broadcasted_iota(I32, (gsz, tr), 0)
    for g in range(N_GROUPS):
        blk = biased[g * gsz:(g + 1) * gsz, :]
        m1 = jnp.max(blk, axis=0, keepdims=True)
        i1 = jnp.min(jnp.where(blk == m1, gio, gsz), axis=0, keepdims=True)
        m2 = jnp.max(jnp.where(gio == i1, neg, blk), axis=0, keepdims=True)
        blocks.append(blk)
        gscore.append(m1 + m2)
    masked = []
    for g in range(N_GROUPS):
        beat = jnp.zeros((1, tr), I32)
        for o in range(N_GROUPS):
            if o == g:
                continue
            wins = (gscore[o] >= gscore[g]) if o < g else (gscore[o] > gscore[g])
            beat = beat + wins.astype(I32)
        masked.append(jnp.where(beat < TOPK_GROUPS, blocks[g], neg))
    masked = jnp.concatenate(masked, axis=0)
    eio = lax.broadcasted_iota(I32, (ne, tr), 0)
    multihot = jnp.zeros((ne, tr), F32)
    idxs, ws = [], []
    for _ in range(TOP_K):
        mx = jnp.max(masked, axis=0, keepdims=True)
        idx = jnp.min(jnp.where(masked == mx, eio, ne), axis=0, keepdims=True)
        hit = eio == idx
        ws.append(jnp.sum(jnp.where(hit, s, 0.0), axis=0, keepdims=True))
        idxs.append(idx)
        masked = jnp.where(hit, neg, masked)
        multihot = multihot + hit.astype(F32)
    wsum = ws[0]
    for k in range(1, TOP_K):
        wsum = wsum + ws[k]
    tr_r = lax.broadcasted_iota(I32, (tr, tr), 0)
    tr_c = lax.broadcasted_iota(I32, (tr, tr), 1)
    before = (tr_r < tr_c).astype(BF16)
    ranks = _dot(multihot.astype(BF16), before) + carry_scr[:, 0:1]
    for k in range(TOP_K):
        eidx_ref[k:k + 1, :] = idxs[k]
        w_ref[k:k + 1, :] = ws[k] / wsum * ROUTED_SCALE
        rank_ref[k:k + 1, :] = jnp.sum(jnp.where(eio == idxs[k], ranks, 0.0), axis=0, keepdims=True).astype(I32)
    carry_scr[...] = carry_scr[...] + jnp.sum(multihot, axis=1, keepdims=True)
    cnt_ref[...] = carry_scr[...]


def _route(lt, bias_col, tr):
    ne, n = lt.shape
    kspec = pl.BlockSpec((TOP_K, tr), lambda i: (0, i))
    return pl.pallas_call(
        _route_kernel,
        out_shape=(_sds((TOP_K, n), I32), _sds((TOP_K, n), F32), _sds((TOP_K, n), I32), _sds((ne, 128), F32)),
        grid=(n // tr,),
        in_specs=[pl.BlockSpec((ne, tr), lambda i: (0, i)), _const_spec(bias_col.shape)],
        out_specs=(kspec, kspec, kspec, pl.BlockSpec((ne, 128), lambda i: (0, 0))),
        scratch_shapes=[pltpu.VMEM((ne, 128), F32)],
        compiler_params=_cp(1),
        name="route",
    )(lt, bias_col)


def _dest_kernel(eidx_ref, rank_ref, pstart_ref, dest_ref):
    ne = pstart_ref.shape[0]
    tr = eidx_ref.shape[1]
    eio = lax.broadcasted_iota(I32, (ne, tr), 0)
    pstart = pstart_ref[...]
    for k in range(TOP_K):
        start = jnp.sum(jnp.where(eio == eidx_ref[k:k + 1, :], pstart, 0.0), axis=0, keepdims=True)
        dest_ref[k:k + 1, :] = start.astype(I32) + rank_ref[k:k + 1, :]


def _dest(eidx, rank, pstart_col, tr):
    n = eidx.shape[1]
    kspec = pl.BlockSpec((TOP_K, tr), lambda i: (0, i))
    return pl.pallas_call(
        _dest_kernel,
        out_shape=_sds((TOP_K, n), I32),
        grid=(n // tr,),
        in_specs=[kspec, kspec, _const_spec(pstart_col.shape)],
        out_specs=kspec,
        compiler_params=_cp(1),
        name="dest",
    )(eidx, rank, pstart_col)


def _dispatch_kernel(dest_ref, u2_ref, xs_hbm, sem, *, td):
    def issue(t, carry):
        src = u2_ref.at[pl.ds(t, 1)]
        for k in range(TOP_K):
            pltpu.make_async_copy(src, xs_hbm.at[pl.ds(dest_ref[t * TOP_K + k], 1)], sem).start(priority=k % 2)
        return carry

    lax.fori_loop(0, td, issue, 0, unroll=2)
    for _ in range(TOP_K):
        pltpu.make_async_copy(u2_ref, xs_hbm.at[pl.ds(0, td)], sem).wait()


def _dispatch(dest_flat, u2, rows_max, td):
    n, d = u2.shape
    return pl.pallas_call(
        functools.partial(_dispatch_kernel, td=td),
        out_shape=_sds((rows_max, d), u2.dtype),
        grid=(n // td,),
        in_specs=[pl.BlockSpec((td * TOP_K,), lambda i: (i,), memory_space=pltpu.SMEM),
                  pl.BlockSpec((td, d), lambda i: (i, 0))],
        out_specs=pl.BlockSpec(memory_space=pl.ANY),
        scratch_shapes=[pltpu.SemaphoreType.DMA],
        compiler_params=_cp(1),
        name="dispatch",
    )(dest_flat, u2)


def _expert_kernel(te_ref, nv_ref, nxt_ref, slot_ref, blk_ref, xs_ref, wg_hbm, wu_hbm, wd_hbm, y_ref,
                   stg_g, stg_u, stg_d, wgb, wub, wdb, sems):
    j = pl.program_id(0)
    n = pl.num_programs(0)
    ne = wg_hbm.shape[0]
    e = te_ref[j]
    nv = nv_ref[j]
    slot = slot_ref[j]
    used = nv > 0
    first = (j == 0) | (te_ref[jnp.maximum(j - 1, 0)] != e)
    jn = jnp.minimum(j + 1, n - 1)
    last = (j == n - 1) | (te_ref[jn] != e) | (nv_ref[jn] == 0)
    has_next = nxt_ref[j] < ne
    nxt = jnp.minimum(nxt_ref[j], ne - 1)

    def weight_copies(expert):
        return (pltpu.make_async_copy(wg_hbm.at[expert], stg_g, sems.at[0]),
                pltpu.make_async_copy(wu_hbm.at[expert], stg_u, sems.at[1]),
                pltpu.make_async_copy(wd_hbm.at[expert], stg_d, sems.at[2]))

    def land(expert, dst_slot):
        for c in weight_copies(expert):
            c.wait()
        wgb[dst_slot] = stg_g[...].astype(BF16)
        wub[dst_slot] = stg_u[...].astype(BF16)
        wdb[dst_slot] = stg_d[...].astype(BF16)

    @pl.when(used & (j == 0))
    def _():
        for c in weight_copies(e):
            c.start()
        land(e, slot)

    @pl.when(used & first & has_next)
    def _():
        for c in weight_copies(nxt):
            c.start(priority=1)

    @pl.when(used)
    def _():
        valid = lax.broadcasted_iota(I32, (xs_ref.shape[0], 1), 0) < nv
        x = jnp.where(valid, xs_ref[...], 0.0).astype(BF16)
        hg = _dot(x, wgb[slot])
        hu = _dot(x, wub[slot])
        y_ref[...] = _dot((_silu(hg) * hu).astype(BF16), wdb[slot])

    @pl.when(used & last & has_next)
    def _():
        land(nxt, 1 - slot)


def _experts(tile_e, tile_nv, tile_nxt, tile_slot, tile_blk, xs, w_gate, w_up, w_down):
    ne, d, f = w_gate.shape
    rows = xs.shape[0]
    tm = MOE_TILE
    xspec = pl.BlockSpec((tm, d), lambda j, te, nv, nx, sl, blk: (blk[j], 0))
    grid_spec = pltpu.PrefetchScalarGridSpec(
        num_scalar_prefetch=5,
        grid=(rows // tm,),
        in_specs=[xspec, pl.BlockSpec(memory_space=pl.ANY), pl.BlockSpec(memory_space=pl.ANY),
                  pl.BlockSpec(memory_space=pl.ANY)],
        out_specs=xspec,
        scratch_shapes=[pltpu.VMEM((d, f), F32), pltpu.VMEM((d, f), F32), pltpu.VMEM((f, d), F32),
                        pltpu.VMEM((2, d, f), BF16), pltpu.VMEM((2, d, f), BF16), pltpu.VMEM((2, f, d), BF16),
                        pltpu.SemaphoreType.DMA((3,))],
    )
    return pl.pallas_call(
        _expert_kernel,
        out_shape=_sds(xs.shape, F32),
        grid_spec=grid_spec,
        compiler_params=_cp(1),
        name="experts",
    )(tile_e, tile_nv, tile_nxt, tile_slot, tile_blk, xs, w_gate, w_up, w_down)


def _final_kernel(dest_ref, dnext_ref, pre_ref, g2_ref, wt_ref, l2w_ref, l2b_ref, ys_hbm, out_ref, buf, sems):
    i = pl.program_id(0)
    n = pl.num_programs(0)
    tm = wt_ref.shape[0]
    cur = i % 2

    def gather(idx_ref, half):
        def issue(t, carry):
            for k in range(TOP_K):
                pltpu.make_async_copy(ys_hbm.at[pl.ds(idx_ref[t * TOP_K + k], 1)],
                                      buf.at[half, k, pl.ds(t, 1)], sems.at[half]).start(priority=k % 2)
            return carry

        lax.fori_loop(0, tm, issue, 0, unroll=2)

    @pl.when(i == 0)
    def _():
        gather(dest_ref, cur)

    @pl.when(i + 1 < n)
    def _():
        gather(dnext_ref, 1 - cur)

    for k in range(TOP_K):
        pltpu.make_async_copy(ys_hbm.at[pl.ds(0, tm)], buf.at[cur, k], sems.at[cur]).wait()
    wt = wt_ref[...]
    routed = buf[cur, 0] * wt[:, 0:1]
    for k in range(1, TOP_K):
        routed = routed + buf[cur, k] * wt[:, k:k + 1]
    pre = pre_ref[...]
    t = pre + (1.0 + g2_ref[...]) * routed.reshape(pre.shape)
    out_ref[...] = _ln(t) * l2w_ref[...] + l2b_ref[...]


def _final(dest, pre, ada3, wt, l2w, l2b, ys, bb, tt, row0):
    b, t, d = pre.shape
    nt = t // tt
    tm = bb * tt
    off = row0 // tm
    xspec = pl.BlockSpec((bb, tt, d), lambda i: (i // nt, i % nt, 0))
    n_steps = (b // bb) * nt
    return pl.pallas_call(
        _final_kernel,
        out_shape=_sds(pre.shape, F32),
        grid=(n_steps,),
        in_specs=[pl.BlockSpec((tm * TOP_K,), lambda i: (off + i,), memory_space=pltpu.SMEM),
                  pl.BlockSpec((tm * TOP_K,), lambda i: (off + jnp.minimum(i + 1, n_steps - 1),),
                               memory_space=pltpu.SMEM),
                  xspec,
                  pl.BlockSpec((bb, 1, d), lambda i: (i // nt, 0, 5)),
                  pl.BlockSpec((tm, TOP_K), lambda i: (off + i, 0)),
                  _const_spec(l2w.shape), _const_spec(l2b.shape),
                  pl.BlockSpec(memory_space=pl.ANY)],
        out_specs=xspec,
        scratch_shapes=[pltpu.VMEM((2, TOP_K, tm, d), F32), pltpu.SemaphoreType.DMA((2,))],
        compiler_params=_cp(1),
        name="final",
    )(dest, dest, pre, ada3, wt, l2w, l2b, ys)


def _tile_t(t, cap):
    return min(t, cap)


def kernel(x_prompt, x_sample, state_hgrn, state_mlstm_c, state_mlstm_n, state_mlstm_m, state_mlstm_conv, c_prompt, c_sample, w_ada, b_ada, w_in, b_in, lb_logits, w_hg_norm, w_conv, b_conv, w_ml_norm, w_proj_a, w_proj_b, w_out, ln1_w, ln1_b, w_router, router_bias, w_exp_gate, w_exp_up, w_exp_down, w_sh_gate, w_sh_up, w_sh_down, ln2_w, ln2_b):
    depth = w_in.shape[0]
    assert depth == 1, "single-layer trunk"
    alpha = (2.0 * depth) ** 0.25
    bp, tp, d = x_prompt.shape
    bs, ts, _ = x_sample.shape
    n_p, n_s = bp * tp, bs * ts
    n_total = n_p + n_s
    ne = w_router.shape[-1]

    w = w_in[0]
    o_hg, o_mqk, o_mv, o_mo = 0, 4 * HG_WIDTH, 4 * HG_WIDTH + ML_QK_WIDTH, 4 * HG_WIDTH + ML_QK_WIDTH + ML_V_WIDTH
    o_mi = o_mo + ML_V_WIDTH
    o_mf = o_mi + ML_HEADS
    o_ga = o_mf + ML_HEADS
    o_gb = o_ga + d
    col_of = {"hq": o_hg, "hf": o_hg + HG_WIDTH, "hi": o_hg + 2 * HG_WIDTH, "hg": o_hg + 3 * HG_WIDTH,
              "mqk": o_mqk, "mv": o_mv, "mo": o_mo}
    regroup = lambda a: jnp.concatenate(
        [a[..., o_ga:o_gb + d]] + [a[..., col_of[k]:col_of[k] + COL] for k in Z16_BLOCKS + Z32_BLOCKS], axis=-1)
    w_main = regroup(w).astype(BF16)
    b_main = regroup(b_in[0])[None, :]
    pad = lambda a: jnp.pad(a, [(0, 0)] * (a.ndim - 1) + [(0, 128 - ML_HEADS)])
    gate_cols = lambda a: jnp.concatenate([pad(a[..., o_mi:o_mf]), pad(a[..., o_mf:o_ga])], axis=-1)
    w_gate = gate_cols(w).astype(BF16)
    b_gate = gate_cols(b_in[0])[None, :]
    wpa, wpb, wo = w_proj_a[0].astype(BF16), w_proj_b[0].astype(BF16), w_out[0].astype(BF16)
    wrt = w_router[0].T
    wsg, wsu, wsd = w_sh_gate[0].astype(BF16), w_sh_up[0].astype(BF16), w_sh_down[0].astype(BF16)
    row2 = lambda a: a[0][None, :]

    n_c = bp + bs
    c_all = jnp.pad(jnp.concatenate([c_prompt, c_sample], axis=0), [(0, -n_c % 8), (0, 0)])
    ada = _ada(c_all, w_ada[0], row2(b_ada))
    ada_p = ada[:bp][:, None, :]
    ada_s = ada[bp:n_c][:, None, :]

    def mixers(x3, ada3, s0, c0, n0, m0, conv0, bb, tt, tt_rec):
        b, t, _ = x3.shape
        z16, z32, gates = _stage_a(x3, ada3, w_main, b_main, w_gate, b_gate, bb, tt)
        o_h, s_new = _hgrn(z16, z32, lb_logits, w_hg_norm, s0, b, t, tt_rec, d)
        h_m, c_new, n_new, m_new, conv_new = _mlstm(z16, z32, gates, conv0, w_conv[0], row2(b_conv),
                                                    row2(w_ml_norm), c0, n0, m0, b, t, tt_rec, d)
        return z16, o_h, h_m, (s_new, c_new, n_new, m_new, conv_new)

    zeros = lambda *s: jnp.zeros(s, F32)
    z_p, oh_p, hm_p, st_p = mixers(x_prompt, ada_p, zeros(bp, HG_HEADS, HG_DK, HG_DV), zeros(bp, ML_HEADS, ML_DV, ML_DK),
                                   zeros(bp, ML_HEADS, ML_DK), zeros(bp, ML_HEADS), zeros(bp, CONV_W - 1, ML_QK_WIDTH),
                                   1, _tile_t(tp, 1024), _tile_t(tp, 512))
    z_s, oh_s, hm_s, st_s = mixers(x_sample, ada_s, state_hgrn[0], state_mlstm_c[0], state_mlstm_n[0],
                                   state_mlstm_m[0], state_mlstm_conv[0], bs, ts, ts)

    tc_p = _tile_t(tp, 256)
    x1_p = _c1(x_prompt, oh_p, hm_p, z_p, ada_p, wpa, wpb, wo, row2(ln1_w), row2(ln1_b), 1, tc_p, alpha)
    x1_s = _c1(x_sample, oh_s, hm_s, z_s, ada_s, wpa, wpb, wo, row2(ln1_w), row2(ln1_b), bs, ts, alpha)
    u2, lt, pre_p = _c2(x1_p, ada_p, wrt, wsg, wsu, wsd, 1, tc_p, alpha, n_total, 0, None)
    u2, lt, pre_s = _c2(x1_s, ada_s, wrt, wsg, wsu, wsd, bs, ts, alpha, n_total, n_p, (u2, lt))

    tok_tile = next(c for c in (512, 256, 128) if n_total % c == 0)
    eidx, wsel, rank, cnt = _route(lt, router_bias[0][:, None], tok_tile)
    counts = cnt[:, 0].astype(I32)
    padded = (counts + MOE_TILE - 1) // MOE_TILE * MOE_TILE
    pend = jnp.cumsum(padded)
    pstart = pend - padded
    dest = _dest(eidx, rank, pstart.astype(F32)[:, None], tok_tile)
    dest = dest.T.reshape(-1)
    rows_max = (n_total * TOP_K + ne * (MOE_TILE - 1)) // MOE_TILE * MOE_TILE
    n_tiles = rows_max // MOE_TILE
    tile_start = jnp.arange(n_tiles, dtype=I32) * MOE_TILE
    owner = (tile_start[:, None] >= pstart[None, :]) & (tile_start[:, None] < pend[None, :])
    tile_nv = jnp.sum(jnp.where(owner, jnp.clip(pstart + counts - tile_start[:, None], 0, MOE_TILE), 0), axis=1)
    tile_e = jnp.minimum(jnp.sum((pend[None, :] <= tile_start[:, None]).astype(I32), axis=1), ne - 1)

    eids = jnp.arange(ne, dtype=I32)
    has_rows = counts > 0
    nxt_e = lax.cummin(jnp.where(has_rows, eids, ne), reverse=True)
    nxt_e = jnp.concatenate([nxt_e[1:], jnp.full((1,), ne, I32)])
    slot_e = (jnp.cumsum(has_rows.astype(I32)) - 1) % 2
    pick = lambda tab: jnp.sum(jnp.where(tile_e[:, None] == eids[None, :], tab[None, :], 0), axis=1).astype(I32)

    xs = _dispatch(dest, u2, rows_max, tok_tile)
    n_used = jnp.sum((tile_nv > 0).astype(I32))
    tile_blk = jnp.minimum(jnp.arange(n_tiles, dtype=I32), jnp.maximum(n_used - 1, 0))
    ys = _experts(tile_e, tile_nv.astype(I32), pick(nxt_e), pick(slot_e), tile_blk, xs,
                  w_exp_gate[0], w_exp_up[0], w_exp_down[0])

    wt = wsel.T
    y_p = _final(dest, pre_p, ada_p, wt, row2(ln2_w), row2(ln2_b), ys, 1, _tile_t(tp, 128), 0)
    y_s = _final(dest, pre_s, ada_s, wt, row2(ln2_w), row2(ln2_b), ys, min(bs, max(1, 128 // ts)), ts, n_p)

    lead = lambda a: a[None]
    return (y_p, y_s) + tuple(lead(a) for a in st_p) + tuple(lead(a) for a in st_s)
```

```python
import functools

import jax
import jax.numpy as jnp
from jax import lax
from jax.experimental import pallas as pl
from jax.experimental.pallas import tpu as pltpu

F32, BF16, I32 = jnp.float32, jnp.bfloat16, jnp.int32
HIGHEST = lax.Precision.HIGHEST
NT_DIMS = (((1,), (1,)), ((), ()))
TN_DIMS = (((0,), (0,)), ((), ()))

CHUNK = 64
SUB = 16
HG_HEADS, HG_DK, HG_DV = 8, 128, 128
ML_HEADS, ML_DK, ML_DV = 4, 128, 256
CONV_W = 4
TOP_K, N_GROUPS, TOPK_GROUPS = 8, 8, 4
ROUTED_SCALE = 2.5
LN_EPS = 1e-5
HG_WIDTH = HG_HEADS * HG_DK
ML_QK_WIDTH = 2 * ML_HEADS * ML_DK
ML_V_WIDTH = ML_HEADS * ML_DV
COL = 1024
GATE_COLS = 256
EXP_CAP = 80.0
MOE_TILE = 384
SUBLANES = 8
STAGE_A_ROWS = 1024
REC_ROWS = 512
C1_ROWS = 256
C2_ROWS = 512
FINAL_ROWS = 128
TOKEN_TILES = (512, 256, 128)
MIB = 1024 * 1024
VMEM_LIMIT = 56 * MIB


def _cp(n_axes, vmem=VMEM_LIMIT):
    return pltpu.CompilerParams(dimension_semantics=("arbitrary",) * n_axes, vmem_limit_bytes=vmem)


def _sds(shape, dtype):
    return jax.ShapeDtypeStruct(shape, dtype)


def _const_spec(shape):
    n = len(shape)
    return pl.BlockSpec(shape, lambda *_: (0,) * n, pipeline_mode=pl.Buffered(1))


def _sigmoid(x):
    return jax.nn.sigmoid(x)


def _silu(x):
    return x * jax.nn.sigmoid(x)


def _ln(x):
    mu = jnp.mean(x, axis=-1, keepdims=True)
    xc = x - mu
    var = jnp.mean(xc * xc, axis=-1, keepdims=True)
    return xc * lax.rsqrt(var + LN_EPS)


def _dot(a, b, **kw):
    return jnp.dot(a, b, preferred_element_type=F32, **kw)


def _dot_nt(a, b, **kw):
    return lax.dot_general(a, b, NT_DIMS, preferred_element_type=F32, **kw)


def _dot_tn(a, b, **kw):
    return lax.dot_general(a, b, TN_DIMS, preferred_element_type=F32, **kw)


def _ada_kernel(c_ref, w_ref, b_ref, o_ref):
    a = _silu(c_ref[...]).astype(BF16)
    o_ref[...] = _dot(a, w_ref[...].astype(BF16)) + b_ref[...]


def _ada(c_all, w_ada, b_ada):
    r, d = c_all.shape
    n6 = w_ada.shape[1]
    tn = 1024
    return pl.pallas_call(
        _ada_kernel,
        out_shape=_sds((r, n6), F32),
        grid=(n6 // tn,),
        in_specs=[pl.BlockSpec((r, d), lambda j: (0, 0)),
                  pl.BlockSpec((d, tn), lambda j: (0, j)),
                  pl.BlockSpec((1, tn), lambda j: (0, j))],
        out_specs=pl.BlockSpec((r, tn), lambda j: (0, j)),
        compiler_params=_cp(1),
        name="ada",
    )(c_all, w_ada, b_ada)


def _stage_a_kernel(x_ref, sc_ref, sh_ref, w_ref, b_ref, wg_ref, bg_ref, z16_ref, z32_ref, g_ref, u_scr, *, n16):
    j = pl.program_id(1)

    @pl.when(j == 0)
    def _():
        u = _ln(x_ref[...]) * (1.0 + sc_ref[...]) + sh_ref[...]
        u2d = u.reshape(u_scr.shape).astype(BF16)
        u_scr[...] = u2d
        g_ref[...] = _dot(u2d, wg_ref[...]) + bg_ref[...]

    @pl.when(j < n16)
    def _():
        z16_ref[...] = (_dot(u_scr[...], w_ref[...]) + b_ref[...]).astype(z16_ref.dtype)

    @pl.when(j >= n16)
    def _():
        z32_ref[...] = _dot(u_scr[...], w_ref[...]) + b_ref[...]


def _stage_a(x3, ada3, w_main, b_main, w_gate, b_gate, bb, tt):
    b, t, d = x3.shape
    nt = t // tt
    tm = bb * tt
    ncol = w_main.shape[1] // COL
    n32 = len(Z32_BLOCKS)
    n16 = ncol - n32
    n = b * t
    return pl.pallas_call(
        functools.partial(_stage_a_kernel, n16=n16),
        out_shape=(_sds((n, n16 * COL), BF16), _sds((n, n32 * COL), F32), _sds((n, GATE_COLS), F32)),
        grid=((b // bb) * nt, ncol),
        in_specs=[pl.BlockSpec((bb, tt, d), lambda i, j: (i // nt, i % nt, 0)),
                  pl.BlockSpec((bb, 1, d), lambda i, j: (i // nt, 0, 1)),
                  pl.BlockSpec((bb, 1, d), lambda i, j: (i // nt, 0, 0)),
                  pl.BlockSpec((d, COL), lambda i, j: (0, j)),
                  pl.BlockSpec((1, COL), lambda i, j: (0, j)),
                  _const_spec((d, GATE_COLS)),
                  _const_spec((1, GATE_COLS))],
        out_specs=(pl.BlockSpec((tm, COL), lambda i, j: (i, jnp.minimum(j, n16 - 1))),
                   pl.BlockSpec((tm, COL), lambda i, j: (i, jnp.maximum(j - n16, 0))),
                   pl.BlockSpec((tm, GATE_COLS), lambda i, j: (i, 0))),
        scratch_shapes=[pltpu.VMEM((tm, d), BF16)],
        compiler_params=_cp(2),
        name="stage_a",
    )(x3, ada3, ada3, w_main, b_main, w_gate, b_gate)


Z16_BLOCKS = ("hq", "hi", "hg", "mv", "mo")
Z32_BLOCKS = ("hf", "mqk")


def _zspec(tt, nt, d_model, name):
    if name in Z32_BLOCKS:
        k = Z32_BLOCKS.index(name)
    else:
        k = 2 * d_model // COL + Z16_BLOCKS.index(name)
    return pl.BlockSpec((tt, COL), lambda bi, j: (bi * nt + j, k))


def _hgrn_kernel(hq_ref, hf_ref, hi_ref, hg_ref, lbl_ref, wn_ref, s0_ref, o_ref, sout_ref, st_scr, *, L, nc):
    j = pl.program_id(1)
    heads = st_scr.shape[0]

    @pl.when(j == 0)
    def _():
        for h in range(heads):
            st_scr[h] = s0_ref[0, h].T

    lbl = lbl_ref[...]
    e = jnp.exp(lbl - jnp.max(lbl, axis=0, keepdims=True))
    lb_all = e[0:1] / jnp.sum(e, axis=0, keepdims=True)
    row = lax.broadcasted_iota(I32, (L, L), 0)
    col = lax.broadcasted_iota(I32, (L, L), 1)
    causal = row >= col
    tri = causal.astype(F32)
    wn = wn_ref[...]

    def chunk(c, carry):
        r0 = pl.multiple_of(c * L, L)
        rows = pl.ds(r0, L)
        sig = _sigmoid(hf_ref[rows, :])
        logf = jnp.log(lb_all + (1.0 - lb_all) * sig)
        key = (1.0 - lb_all) * (1.0 - sig)
        q = _silu(hq_ref[rows, :].astype(F32))
        g = _dot(tri, logf, precision=HIGHEST)
        g_last = g[L - 1:L, :]
        qdec = (q * jnp.exp(g)).astype(BF16)
        khat = (key * jnp.exp(g_last - g)).astype(BF16)
        decay = jnp.exp(g_last)
        vb = hi_ref[rows, :].astype(BF16)
        qsub, ksub = [], []
        for i in range(L // SUB):
            sub = slice(i * SUB, (i + 1) * SUB)
            gref = g[i * SUB + SUB // 2:i * SUB + SUB // 2 + 1, :]
            qsub.append((q[sub] * jnp.exp(g[sub] - gref)).astype(BF16))
            ksub.append((key * jnp.exp(jnp.minimum(gref - g, EXP_CAP))).astype(BF16))
        o_inter, att, upd = [], [], []
        for h in range(heads):
            hs = slice(h * HG_DK, (h + 1) * HG_DK)
            o_inter.append(_dot_nt(qdec[:, hs], st_scr[h].astype(BF16)))
            blocks = [_dot_nt(qs[:, hs], ks[:, hs]) for qs, ks in zip(qsub, ksub)]
            att.append(blocks[0] if len(blocks) == 1 else jnp.concatenate(blocks, axis=0))
            upd.append(_dot_tn(vb[:, hs], khat[:, hs]))
        for h in range(heads):
            hs = slice(h * HG_DK, (h + 1) * HG_DK)
            o = o_inter[h] + _dot(jnp.where(causal, att[h], 0.0).astype(BF16), vb[:, hs])
            st_scr[h] = st_scr[h] * decay[:, hs] + upd[h]
            o = o * lax.rsqrt(jnp.mean(o * o, axis=-1, keepdims=True) + LN_EPS) * wn
            o_ref[rows, hs] = (o * _silu(hg_ref[rows, hs].astype(F32))).astype(o_ref.dtype)
        return carry

    lax.fori_loop(0, nc, chunk, 0)

    @pl.when(j == pl.num_programs(1) - 1)
    def _():
        for h in range(heads):
            sout_ref[0, h] = st_scr[h].T


def _hgrn(z16, z32, lb_logits, w_norm, s0, b, t, tt, d_model):
    L = min(CHUNK, t)
    nt = t // tt
    zspec = functools.partial(_zspec, tt, nt, d_model)
    return pl.pallas_call(
        functools.partial(_hgrn_kernel, L=L, nc=tt // L),
        out_shape=(_sds((b * t, HG_WIDTH), BF16), _sds(s0.shape, F32)),
        grid=(b, nt),
        in_specs=[zspec("hq"), zspec("hf"), zspec("hi"), zspec("hg"),
                  _const_spec(lb_logits.shape), _const_spec(w_norm.shape),
                  pl.BlockSpec((1,) + s0.shape[1:], lambda bi, j: (bi, 0, 0, 0))],
        out_specs=(pl.BlockSpec((tt, HG_WIDTH), lambda bi, j: (bi * nt + j, 0)),
                   pl.BlockSpec((1,) + s0.shape[1:], lambda bi, j: (bi, 0, 0, 0))),
        scratch_shapes=[pltpu.VMEM((HG_HEADS, HG_DV, HG_DK), F32)],
        compiler_params=_cp(2),
        name="hgrn",
    )(z16, z32, z16, z16, lb_logits, w_norm, s0)


def _mlstm_kernel(qk_ref, v_ref, mo_ref, g_ref, conv0_ref, wc_ref, bc_ref, wn_ref, c0_ref, n0_ref, m0_ref,
                  o_ref, cout_ref, nout_ref, mout_ref, convout_ref,
                  xbuf, qk_scr, c_scr, n_scr, m_scr, *, L, nc):
    j = pl.program_id(1)
    last = pl.num_programs(1) - 1
    heads = c_scr.shape[0]
    tt = qk_ref.shape[0]
    P = CONV_W - 1
    OFF = 8
    QW = heads * ML_DK

    @pl.when(j == 0)
    def _():
        xbuf[OFF - P:OFF, :] = conv0_ref[0]
        c_scr[...] = c0_ref[0]
        n_scr[0:heads, :] = n0_ref[0]
        m0 = m0_ref[0]
        for h in range(heads):
            m_scr[h:h + 1, :] = jnp.broadcast_to(m0[:, h:h + 1], (1, 128))

    xbuf[OFF:OFF + tt, :] = qk_ref[...]
    acc = bc_ref[...] + wc_ref[0:1, :] * xbuf[OFF - P:OFF - P + tt, :]
    for tap in range(1, CONV_W):
        acc = acc + wc_ref[tap:tap + 1, :] * xbuf[OFF - P + tap:OFF - P + tap + tt, :]
    qk = _silu(acc)
    qk_scr[:, 0:QW] = qk[:, 0:QW]
    qk_scr[:, QW:2 * QW] = qk[:, QW:2 * QW] * (ML_DK ** -0.5)

    @pl.when(j == last)
    def _():
        convout_ref[0] = xbuf[OFF + tt - P:OFF + tt, :]

    xbuf[OFF - P:OFF, :] = xbuf[OFF + tt - P:OFF + tt, :]

    row = lax.broadcasted_iota(I32, (L, L), 0)
    col = lax.broadcasted_iota(I32, (L, L), 1)
    causal = row >= col
    tri = causal.astype(F32)
    wn = wn_ref[...]

    def chunk(c, carry):
        r0 = pl.multiple_of(c * L, L)
        rows = pl.ds(r0, L)
        gates = g_ref[rows, :]
        gi = gates[:, 0:128]
        gf = gates[:, 128:256]
        lf = jnp.minimum(gf, 0.0) - jnp.log(1.0 + jnp.exp(-jnp.abs(gf)))
        bcum = _dot(tri, lf, precision=HIGHEST)
        a = gi - bcum
        a_t = a.T
        for h in range(heads):
            q = qk_scr[rows, h * ML_DK:(h + 1) * ML_DK]
            k = qk_scr[rows, QW + h * ML_DK:QW + (h + 1) * ML_DK]
            vb = v_ref[rows, h * ML_DV:(h + 1) * ML_DV].astype(BF16)
            b_col = bcum[:, h:h + 1]
            a_col = a[:, h:h + 1]
            a_row = a_t[h:h + 1, :]
            m_prev = m_scr[h:h + 1, 0:1]
            dlog = jnp.where(causal, b_col + a_row, -jnp.inf)
            inter_log = b_col + m_prev
            m_t = jnp.maximum(inter_log, jnp.max(dlog, axis=-1, keepdims=True))
            w = jnp.exp(dlog - m_t)
            inter_w = jnp.exp(inter_log - m_t)
            qb, kb = q.astype(BF16), k.astype(BF16)
            sc = _dot_nt(qb, kb) * w
            cst = c_scr[h]
            num = inter_w * _dot_nt(qb, cst.astype(BF16)) + _dot(sc.astype(BF16), vb)
            nrow = n_scr[h:h + 1, :]
            den = inter_w * jnp.sum(q * nrow, axis=-1, keepdims=True) + jnp.sum(sc, axis=-1, keepdims=True)
            hh = num / jnp.maximum(jnp.abs(den), jnp.exp(-m_t))
            b_last = b_col[L - 1:L, :]
            m_new = m_t[L - 1:L, :]
            decay = jnp.exp(b_last + m_prev - m_new)
            kw = k * jnp.exp(b_last + a_col - m_new)
            c_scr[h] = decay * cst + _dot_tn(vb, kw.astype(BF16))
            n_scr[h:h + 1, :] = decay * nrow + jnp.sum(kw, axis=0, keepdims=True)
            m_scr[h:h + 1, :] = jnp.broadcast_to(m_new, (1, 128))
            mu = jnp.mean(hh, axis=-1, keepdims=True)
            hc = hh - mu
            hn = hc * lax.rsqrt(jnp.mean(hc * hc, axis=-1, keepdims=True) + LN_EPS)
            hn = hn * wn[:, h * ML_DV:(h + 1) * ML_DV]
            og = _sigmoid(mo_ref[rows, h * ML_DV:(h + 1) * ML_DV].astype(F32))
            o_ref[rows, h * ML_DV:(h + 1) * ML_DV] = (hn * og).astype(o_ref.dtype)
        return carry

    lax.fori_loop(0, nc, chunk, 0)

    @pl.when(j == last)
    def _():
        cout_ref[0] = c_scr[...]
        nout_ref[0] = n_scr[0:heads, :]
        lane = lax.broadcasted_iota(I32, (1, 128), 1)
        mrow = jnp.zeros((1, 128), F32)
        for h in range(heads):
            mrow = jnp.where(lane == h, m_scr[h:h + 1, :], mrow)
        mout_ref[0] = mrow[:, 0:heads]


def _mlstm(z16, z32, gates, conv0, w_conv, b_conv, w_norm, c0, n0, m0, b, t, tt, d_model):
    L = min(CHUNK, t)
    nt = t // tt
    m0 = m0.reshape(b, 1, ML_HEADS)
    zspec = functools.partial(_zspec, tt, nt, d_model)
    bspec = lambda a: pl.BlockSpec((1,) + a.shape[1:], lambda bi, j: (bi,) + (0,) * (a.ndim - 1))
    outs = pl.pallas_call(
        functools.partial(_mlstm_kernel, L=L, nc=tt // L),
        out_shape=(_sds((b * t, ML_V_WIDTH), BF16), _sds(c0.shape, F32), _sds(n0.shape, F32),
                   _sds(m0.shape, F32), _sds(conv0.shape, F32)),
        grid=(b, nt),
        in_specs=[zspec("mqk"), zspec("mv"), zspec("mo"),
                  pl.BlockSpec((tt, GATE_COLS), lambda bi, j: (bi * nt + j, 0)),
                  bspec(conv0), _const_spec(w_conv.shape), _const_spec(b_conv.shape), _const_spec(w_norm.shape),
                  bspec(c0), bspec(n0), bspec(m0)],
        out_specs=(pl.BlockSpec((tt, ML_V_WIDTH), lambda bi, j: (bi * nt + j, 0)),
                   bspec(c0), bspec(n0), bspec(m0), bspec(conv0)),
        scratch_shapes=[pltpu.VMEM((tt + 8, ML_QK_WIDTH), F32), pltpu.VMEM((tt, ML_QK_WIDTH), F32),
                        pltpu.VMEM((ML_HEADS, ML_DV, ML_DK), F32), pltpu.VMEM((8, 128), F32),
                        pltpu.VMEM((8, 128), F32)],
        compiler_params=_cp(2),
        name="mlstm",
    )(z32, z16, z16, gates, conv0, w_conv, b_conv, w_norm, c0, n0, m0)
    h_m, c_new, n_new, m_new, conv_new = outs
    return h_m, c_new, n_new, m_new.reshape(b, ML_HEADS), conv_new


def _c1_kernel(x_ref, oh_ref, hm_ref, ga_ref, gb_ref, g1_ref, wpa_ref, wpb_ref, wo_ref, l1w_ref, l1b_ref,
               x1_ref, *, alpha):
    a = _dot(oh_ref[...], wpa_ref[...])
    b = _dot(hm_ref[...], wpb_ref[...])
    merged = _sigmoid(ga_ref[...].astype(F32)) * a + _sigmoid(gb_ref[...].astype(F32)) * b
    y = _dot(merged.astype(BF16), wo_ref[...])
    x = x_ref[...]
    t = alpha * x + (1.0 + g1_ref[...]) * y.reshape(x.shape)
    x1_ref[...] = _ln(t) * l1w_ref[...] + l1b_ref[...]


def _c1(x3, o_h, h_m, z, ada3, wpa, wpb, wo, l1w, l1b, bb, tt, alpha):
    b, t, d = x3.shape
    nt = t // tt
    tm = bb * tt
    xspec = pl.BlockSpec((bb, tt, d), lambda i: (i // nt, i % nt, 0))
    return pl.pallas_call(
        functools.partial(_c1_kernel, alpha=alpha),
        out_shape=_sds(x3.shape, F32),
        grid=((b // bb) * nt,),
        in_specs=[xspec,
                  pl.BlockSpec((tm, HG_WIDTH), lambda i: (i, 0)),
                  pl.BlockSpec((tm, ML_V_WIDTH), lambda i: (i, 0)),
                  pl.BlockSpec((tm, d), lambda i: (i, 0)),
                  pl.BlockSpec((tm, d), lambda i: (i, 1)),
                  pl.BlockSpec((bb, 1, d), lambda i: (i // nt, 0, 2)),
                  _const_spec(wpa.shape), _const_spec(wpb.shape), _const_spec(wo.shape),
                  _const_spec(l1w.shape), _const_spec(l1b.shape)],
        out_specs=xspec,
        compiler_params=_cp(1),
        name="c1",
    )(x3, o_h, h_m, z, z, ada3, wpa, wpb, wo, l1w, l1b)


def _c2_kernel(x1_ref, sh2_ref, sc2_ref, g2_ref, wrt_ref, wsg_ref, wsu_ref, wsd_ref, *rest, alpha):
    u2_ref, lt_ref, pre_ref = rest[-3:]
    x1 = x1_ref[...]
    u = _ln(x1) * (1.0 + sc2_ref[...]) + sh2_ref[...]
    u2d = u.reshape(u2_ref.shape)
    u2_ref[...] = u2d
    lt_ref[...] = _dot_nt(wrt_ref[...], u2d, precision=HIGHEST)
    ub = u2d.astype(BF16)
    hg = _dot(ub, wsg_ref[...])
    hu = _dot(ub, wsu_ref[...])
    shared = _dot((_silu(hg) * hu).astype(BF16), wsd_ref[...])
    pre_ref[...] = alpha * x1 + (1.0 + g2_ref[...]) * shared.reshape(x1.shape)


def _c2(x1, ada3, wrt, wsg, wsu, wsd, bb, tt, alpha, n_total, row0, shared_bufs):
    b, t, d = x1.shape
    nt = t // tt
    tm = bb * tt
    ne = wrt.shape[0]
    off = row0 // tm
    xspec = pl.BlockSpec((bb, tt, d), lambda i: (i // nt, i % nt, 0))
    mod = lambda c: pl.BlockSpec((bb, 1, d), lambda i: (i // nt, 0, c))
    in_specs = [xspec, mod(3), mod(4), mod(5),
                _const_spec(wrt.shape), _const_spec(wsg.shape), _const_spec(wsu.shape), _const_spec(wsd.shape)]
    args = [x1, ada3, ada3, ada3, wrt, wsg, wsu, wsd]
    aliases = {}
    if shared_bufs is not None:
        in_specs += [pl.BlockSpec(memory_space=pl.ANY), pl.BlockSpec(memory_space=pl.ANY)]
        aliases = {len(args): 0, len(args) + 1: 1}
        args += list(shared_bufs)
    return pl.pallas_call(
        functools.partial(_c2_kernel, alpha=alpha),
        out_shape=(_sds((n_total, d), F32), _sds((ne, n_total), F32), _sds(x1.shape, F32)),
        grid=((b // bb) * nt,),
        in_specs=in_specs,
        out_specs=(pl.BlockSpec((tm, d), lambda i: (off + i, 0)),
                   pl.BlockSpec((ne, tm), lambda i: (0, off + i)),
                   xspec),
        input_output_aliases=aliases,
        compiler_params=_cp(1),
        name="c2",
    )(*args)


def _route_kernel(lt_ref, bias_ref, eidx_ref, w_ref, rank_ref, cnt_ref, carry_scr):
    @pl.when(pl.program_id(0) == 0)
    def _():
        carry_scr[...] = jnp.zeros_like(carry_scr)

    ne, tr = lt_ref.shape
    gsz = ne // N_GROUPS
    s = _sigmoid(lt_ref[...])
    biased = s + bias_ref[...]
    neg = -jnp.inf
    blocks, gscore = [], []
    gio = lax.broadcasted_iota(I32, (gsz, tr), 0)
    for g in range(N_GROUPS):
        blk = biased[g * gsz:(g + 1) * gsz, :]
        m1 = jnp.max(blk, axis=0, keepdims=True)
        i1 = jnp.min(jnp.where(blk == m1, gio, gsz), axis=0, keepdims=True)
        m2 = jnp.max(jnp.where(gio == i1, neg, blk), axis=0, keepdims=True)
        blocks.append(blk)
        gscore.append(m1 + m2)
    masked = []
    for g in range(N_GROUPS):
        beat = jnp.zeros((1, tr), I32)
        for o in range(N_GROUPS):
            if o == g:
                continue
            wins = (gscore[o] >= gscore[g]) if o < g else (gscore[o] > gscore[g])
            beat = beat + wins.astype(I32)
        masked.append(jnp.where(beat < TOPK_GROUPS, blocks[g], neg))
    masked = jnp.concatenate(masked, axis=0)
    eio = lax.broadcasted_iota(I32, (ne, tr), 0)
    multihot = jnp.zeros((ne, tr), F32)
    idxs, ws = [], []
    for _ in range(TOP_K):
        mx = jnp.max(masked, axis=0, keepdims=True)
        idx = jnp.min(jnp.where(masked == mx, eio, ne), axis=0, keepdims=True)
        hit = eio == idx
        ws.append(jnp.sum(jnp.where(hit, s, 0.0), axis=0, keepdims=True))
        idxs.append(idx)
        masked = jnp.where(hit, neg, masked)
        multihot = multihot + hit.astype(F32)
    wsum = ws[0]
    for k in range(1, TOP_K):
        wsum = wsum + ws[k]
    tr_r = lax.broadcasted_iota(I32, (tr, tr), 0)
    tr_c = lax.broadcasted_iota(I32, (tr, tr), 1)
    before = (tr_r < tr_c).astype(BF16)
    ranks = _dot(multihot.astype(BF16), before) + carry_scr[:, 0:1]
    for k in range(TOP_K):
        eidx_ref[k:k + 1, :] = idxs[k]
        w_ref[k:k + 1, :] = ws[k] / wsum * ROUTED_SCALE
        rank_ref[k:k + 1, :] = jnp.sum(jnp.where(eio == idxs[k], ranks, 0.0), axis=0, keepdims=True).astype(I32)
    carry_scr[...] = carry_scr[...] + jnp.sum(multihot, axis=1, keepdims=True)
    cnt_ref[...] = carry_scr[...]


def _route(lt, bias_col, tr):
    ne, n = lt.shape
    kspec = pl.BlockSpec((TOP_K, tr), lambda i: (0, i))
    return pl.pallas_call(
        _route_kernel,
        out_shape=(_sds((TOP_K, n), I32), _sds((TOP_K, n), F32), _sds((TOP_K, n), I32), _sds((ne, 128), F32)),
        grid=(n // tr,),
        in_specs=[pl.BlockSpec((ne, tr), lambda i: (0, i)), _const_spec(bias_col.shape)],
        out_specs=(kspec, kspec, kspec, pl.BlockSpec((ne, 128), lambda i: (0, 0))),
        scratch_shapes=[pltpu.VMEM((ne, 128), F32)],
        compiler_params=_cp(1),
        name="route",
    )(lt, bias_col)


def _dest_kernel(eidx_ref, rank_ref, pstart_ref, dest_ref):
    ne = pstart_ref.shape[0]
    tr = eidx_ref.shape[1]
    eio = lax.broadcasted_iota(I32, (ne, tr), 0)
    pstart = pstart_ref[...]
    for k in range(TOP_K):
        start = jnp.sum(jnp.where(eio == eidx_ref[k:k + 1, :], pstart, 0.0), axis=0, keepdims=True)
        dest_ref[k:k + 1, :] = start.astype(I32) + rank_ref[k:k + 1, :]


def _dest(eidx, rank, pstart_col, tr):
    n = eidx.shape[1]
    kspec = pl.BlockSpec((TOP_K, tr), lambda i: (0, i))
    return pl.pallas_call(
        _dest_kernel,
        out_shape=_sds((TOP_K, n), I32),
        grid=(n // tr,),
        in_specs=[kspec, kspec, _const_spec(pstart_col.shape)],
        out_specs=kspec,
        compiler_params=_cp(1),
        name="dest",
    )(eidx, rank, pstart_col)


def _dispatch_kernel(dest_ref, u2_ref, xs_hbm, sem, *, td):
    def issue(g, carry):
        t0 = pl.multiple_of(g * SUBLANES, SUBLANES)
        for u in range(SUBLANES):
            src = u2_ref.at[pl.ds(t0 + u, 1)]
            for k in range(TOP_K):
                row = dest_ref[t0 * TOP_K + (u * TOP_K + k)]
                pltpu.make_async_copy(src, xs_hbm.at[pl.ds(row, 1)], sem).start(priority=k % 2)
        return carry

    lax.fori_loop(0, td // SUBLANES, issue, 0)
    for _ in range(TOP_K):
        pltpu.make_async_copy(u2_ref, xs_hbm.at[pl.ds(0, td)], sem).wait()


def _dispatch(dest_flat, u2, rows_max, td):
    n, d = u2.shape
    return pl.pallas_call(
        functools.partial(_dispatch_kernel, td=td),
        out_shape=_sds((rows_max, d), u2.dtype),
        grid=(n // td,),
        in_specs=[pl.BlockSpec((td * TOP_K,), lambda i: (i,), memory_space=pltpu.SMEM),
                  pl.BlockSpec((td, d), lambda i: (i, 0))],
        out_specs=pl.BlockSpec(memory_space=pl.ANY),
        scratch_shapes=[pltpu.SemaphoreType.DMA],
        compiler_params=_cp(1),
        name="dispatch",
    )(dest_flat, u2)


def _expert_kernel(te_ref, nv_ref, nxt_ref, slot_ref, blk_ref, xs_ref, wg_hbm, wu_hbm, wd_hbm, y_ref,
                   stg_g, stg_u, stg_d, wgb, wub, wdb, sems):
    j = pl.program_id(0)
    n = pl.num_programs(0)
    ne = wg_hbm.shape[0]
    e = te_ref[j]
    nv = nv_ref[j]
    slot = slot_ref[j]
    used = nv > 0
    first = (j == 0) | (te_ref[jnp.maximum(j - 1, 0)] != e)
    jn = jnp.minimum(j + 1, n - 1)
    last = (j == n - 1) | (te_ref[jn] != e) | (nv_ref[jn] == 0)
    has_next = nxt_ref[j] < ne
    nxt = jnp.minimum(nxt_ref[j], ne - 1)

    def weight_copies(expert):
        return (pltpu.make_async_copy(wg_hbm.at[expert], stg_g, sems.at[0]),
                pltpu.make_async_copy(wu_hbm.at[expert], stg_u, sems.at[1]),
                pltpu.make_async_copy(wd_hbm.at[expert], stg_d, sems.at[2]))

    def land(expert, dst_slot):
        for c in weight_copies(expert):
            c.wait()
        wgb[dst_slot] = stg_g[...].astype(BF16)
        wub[dst_slot] = stg_u[...].astype(BF16)
        wdb[dst_slot] = stg_d[...].astype(BF16)

    @pl.when(used & (j == 0))
    def _():
        for c in weight_copies(e):
            c.start()
        land(e, slot)

    @pl.when(used & first & has_next)
    def _():
        for c in weight_copies(nxt):
            c.start(priority=1)

    @pl.when(used)
    def _():
        valid = lax.broadcasted_iota(I32, (xs_ref.shape[0], 1), 0) < nv
        x = jnp.where(valid, xs_ref[...], 0.0).astype(BF16)
        hg = _dot(x, wgb[slot])
        hu = _dot(x, wub[slot])
        y_ref[...] = _dot((_silu(hg) * hu).astype(BF16), wdb[slot])

    @pl.when(used & last & has_next)
    def _():
        land(nxt, 1 - slot)


def _experts(tile_e, tile_nv, tile_nxt, tile_slot, tile_blk, xs, w_gate, w_up, w_down):
    ne, d, f = w_gate.shape
    rows = xs.shape[0]
    tm = MOE_TILE
    xspec = pl.BlockSpec((tm, d), lambda j, te, nv, nx, sl, blk: (blk[j], 0))
    grid_spec = pltpu.PrefetchScalarGridSpec(
        num_scalar_prefetch=5,
        grid=(rows // tm,),
        in_specs=[xspec, pl.BlockSpec(memory_space=pl.ANY), pl.BlockSpec(memory_space=pl.ANY),
                  pl.BlockSpec(memory_space=pl.ANY)],
        out_specs=xspec,
        scratch_shapes=[pltpu.VMEM((d, f), F32), pltpu.VMEM((d, f), F32), pltpu.VMEM((f, d), F32),
                        pltpu.VMEM((2, d, f), BF16), pltpu.VMEM((2, d, f), BF16), pltpu.VMEM((2, f, d), BF16),
                        pltpu.SemaphoreType.DMA((3,))],
    )
    return pl.pallas_call(
        _expert_kernel,
        out_shape=_sds(xs.shape, F32),
        grid_spec=grid_spec,
        compiler_params=_cp(1),
        name="experts",
    )(tile_e, tile_nv, tile_nxt, tile_slot, tile_blk, xs, w_gate, w_up, w_down)


def _final_kernel(dest_ref, dnext_ref, pre_ref, g2_ref, wt_ref, l2w_ref, l2b_ref, ys_hbm, out_ref,
                  buf_a, buf_b, sems):
    i = pl.program_id(0)
    n = pl.num_programs(0)
    tm = wt_ref.shape[0]
    tt = pre_ref.shape[1]

    def row_copy(idx_ref, t0, u, k, dst, sem):
        row = idx_ref[t0 * TOP_K + (u * TOP_K + k)]
        return pltpu.make_async_copy(ys_hbm.at[pl.ds(row, 1)], dst.at[k, pl.ds(t0 + u, 1)], sem)

    def wait_all(dst, sem):
        for k in range(TOP_K):
            pltpu.make_async_copy(ys_hbm.at[pl.ds(0, tm)], dst.at[k], sem).wait()

    def step(cur, cur_sem, nxt, nxt_sem):
        @pl.when(i == 0)
        def _():
            def issue(g, carry):
                t0 = pl.multiple_of(g * SUBLANES, SUBLANES)
                for u in range(SUBLANES):
                    for k in range(TOP_K):
                        row_copy(dest_ref, t0, u, k, cur, cur_sem).start(priority=k % 2)
                return carry

            lax.fori_loop(0, tm // SUBLANES, issue, 0)

        wait_all(cur, cur_sem)
        wt = wt_ref[...]
        l2w, l2b = l2w_ref[...], l2b_ref[...]
        for g in range(tm // SUBLANES):
            t0 = g * SUBLANES
            for u in range(SUBLANES):
                for k in range(TOP_K):
                    row_copy(dnext_ref, t0, u, k, nxt, nxt_sem).start(priority=k % 2)
            rows = slice(t0, t0 + SUBLANES)
            routed = cur[0, rows, :] * wt[rows, 0:1]
            for k in range(1, TOP_K):
                routed = routed + cur[k, rows, :] * wt[rows, k:k + 1]
            bi, r0 = divmod(t0, tt)
            tval = pre_ref[bi, r0:r0 + SUBLANES, :] + (1.0 + g2_ref[bi]) * routed
            out_ref[bi, r0:r0 + SUBLANES, :] = _ln(tval) * l2w + l2b

        @pl.when(i == n - 1)
        def _():
            wait_all(nxt, nxt_sem)

    @pl.when(i % 2 == 0)
    def _():
        step(buf_a, sems.at[0], buf_b, sems.at[1])

    @pl.when(i % 2 == 1)
    def _():
        step(buf_b, sems.at[1], buf_a, sems.at[0])


def _final(dest, pre, ada3, wt, l2w, l2b, ys, bb, tt, row0):
    b, t, d = pre.shape
    nt = t // tt
    tm = bb * tt
    off = row0 // tm
    xspec = pl.BlockSpec((bb, tt, d), lambda i: (i // nt, i % nt, 0))
    n_steps = (b // bb) * nt
    return pl.pallas_call(
        _final_kernel,
        out_shape=_sds(pre.shape, F32),
        grid=(n_steps,),
        in_specs=[pl.BlockSpec((tm * TOP_K,), lambda i: (off + i,), memory_space=pltpu.SMEM),
                  pl.BlockSpec((tm * TOP_K,), lambda i: (off + jnp.minimum(i + 1, n_steps - 1),),
                               memory_space=pltpu.SMEM),
                  xspec,
                  pl.BlockSpec((bb, 1, d), lambda i: (i // nt, 0, 5)),
                  pl.BlockSpec((tm, TOP_K), lambda i: (off + i, 0)),
                  _const_spec(l2w.shape), _const_spec(l2b.shape),
                  pl.BlockSpec(memory_space=pl.ANY)],
        out_specs=xspec,
        scratch_shapes=[pltpu.VMEM((TOP_K, tm, d), F32), pltpu.VMEM((TOP_K, tm, d), F32),
                        pltpu.SemaphoreType.DMA((2,))],
        compiler_params=_cp(1),
        name="final",
    )(dest, dest, pre, ada3, wt, l2w, l2b, ys)


def _tile_t(t, cap):
    return min(t, cap)


def kernel(x_prompt, x_sample, state_hgrn, state_mlstm_c, state_mlstm_n, state_mlstm_m, state_mlstm_conv, c_prompt, c_sample, w_ada, b_ada, w_in, b_in, lb_logits, w_hg_norm, w_conv, b_conv, w_ml_norm, w_proj_a, w_proj_b, w_out, ln1_w, ln1_b, w_router, router_bias, w_exp_gate, w_exp_up, w_exp_down, w_sh_gate, w_sh_up, w_sh_down, ln2_w, ln2_b):
    depth = w_in.shape[0]
    assert depth == 1, "single-layer trunk"
    alpha = (2.0 * depth) ** 0.25
    bp, tp, d = x_prompt.shape
    bs, ts, _ = x_sample.shape
    n_p, n_s = bp * tp, bs * ts
    n_total = n_p + n_s
    ne = w_router.shape[-1]

    w = w_in[0]
    o_hg, o_mqk, o_mv, o_mo = 0, 4 * HG_WIDTH, 4 * HG_WIDTH + ML_QK_WIDTH, 4 * HG_WIDTH + ML_QK_WIDTH + ML_V_WIDTH
    o_mi = o_mo + ML_V_WIDTH
    o_mf = o_mi + ML_HEADS
    o_ga = o_mf + ML_HEADS
    o_gb = o_ga + d
    col_of = {"hq": o_hg, "hf": o_hg + HG_WIDTH, "hi": o_hg + 2 * HG_WIDTH, "hg": o_hg + 3 * HG_WIDTH,
              "mqk": o_mqk, "mv": o_mv, "mo": o_mo}
    regroup = lambda a: jnp.concatenate(
        [a[..., o_ga:o_gb + d]] + [a[..., col_of[k]:col_of[k] + COL] for k in Z16_BLOCKS + Z32_BLOCKS], axis=-1)
    w_main = regroup(w).astype(BF16)
    b_main = regroup(b_in[0])[None, :]
    pad = lambda a: jnp.pad(a, [(0, 0)] * (a.ndim - 1) + [(0, 128 - ML_HEADS)])
    gate_cols = lambda a: jnp.concatenate([pad(a[..., o_mi:o_mf]), pad(a[..., o_mf:o_ga])], axis=-1)
    w_gate = gate_cols(w).astype(BF16)
    b_gate = gate_cols(b_in[0])[None, :]
    wpa, wpb, wo = w_proj_a[0].astype(BF16), w_proj_b[0].astype(BF16), w_out[0].astype(BF16)
    wrt = w_router[0].T
    wsg, wsu, wsd = w_sh_gate[0].astype(BF16), w_sh_up[0].astype(BF16), w_sh_down[0].astype(BF16)
    row2 = lambda a: a[0][None, :]

    n_c = bp + bs
    c_all = jnp.pad(jnp.concatenate([c_prompt, c_sample], axis=0), [(0, -n_c % 8), (0, 0)])
    ada = _ada(c_all, w_ada[0], row2(b_ada))
    ada_p = ada[:bp][:, None, :]
    ada_s = ada[bp:n_c][:, None, :]

    def mixers(x3, ada3, s0, c0, n0, m0, conv0, bb, tt, tt_rec):
        b, t, _ = x3.shape
        z16, z32, gates = _stage_a(x3, ada3, w_main, b_main, w_gate, b_gate, bb, tt)
        o_h, s_new = _hgrn(z16, z32, lb_logits, w_hg_norm, s0, b, t, tt_rec, d)
        h_m, c_new, n_new, m_new, conv_new = _mlstm(z16, z32, gates, conv0, w_conv[0], row2(b_conv),
                                                    row2(w_ml_norm), c0, n0, m0, b, t, tt_rec, d)
        return z16, o_h, h_m, (s_new, c_new, n_new, m_new, conv_new)

    zeros = lambda *s: jnp.zeros(s, F32)
    z_p, oh_p, hm_p, st_p = mixers(x_prompt, ada_p, zeros(bp, HG_HEADS, HG_DK, HG_DV), zeros(bp, ML_HEADS, ML_DV, ML_DK),
                                   zeros(bp, ML_HEADS, ML_DK), zeros(bp, ML_HEADS), zeros(bp, CONV_W - 1, ML_QK_WIDTH),
                                   1, _tile_t(tp, STAGE_A_ROWS), _tile_t(tp, REC_ROWS))
    z_s, oh_s, hm_s, st_s = mixers(x_sample, ada_s, state_hgrn[0], state_mlstm_c[0], state_mlstm_n[0],
                                   state_mlstm_m[0], state_mlstm_conv[0], bs, ts, ts)

    tc_p = _tile_t(tp, C1_ROWS)
    x1_p = _c1(x_prompt, oh_p, hm_p, z_p, ada_p, wpa, wpb, wo, row2(ln1_w), row2(ln1_b), 1, tc_p, alpha)
    x1_s = _c1(x_sample, oh_s, hm_s, z_s, ada_s, wpa, wpb, wo, row2(ln1_w), row2(ln1_b), bs, ts, alpha)
    u2, lt, pre_p = _c2(x1_p, ada_p, wrt, wsg, wsu, wsd, 1, _tile_t(tp, C2_ROWS), alpha, n_total, 0, None)
    u2, lt, pre_s = _c2(x1_s, ada_s, wrt, wsg, wsu, wsd, bs, ts, alpha, n_total, n_p, (u2, lt))

    tok_tile = next(c for c in TOKEN_TILES if n_total % c == 0)
    eidx, wsel, rank, cnt = _route(lt, router_bias[0][:, None], tok_tile)
    counts = cnt[:, 0].astype(I32)
    padded = (counts + MOE_TILE - 1) // MOE_TILE * MOE_TILE
    pend = jnp.cumsum(padded)
    pstart = pend - padded
    dest = _dest(eidx, rank, pstart.astype(F32)[:, None], tok_tile)
    dest = dest.T.reshape(-1)
    rows_max = (n_total * TOP_K + ne * (MOE_TILE - 1)) // MOE_TILE * MOE_TILE
    n_tiles = rows_max // MOE_TILE
    tile_start = jnp.arange(n_tiles, dtype=I32) * MOE_TILE
    owner = (tile_start[:, None] >= pstart[None, :]) & (tile_start[:, None] < pend[None, :])
    tile_nv = jnp.sum(jnp.where(owner, jnp.clip(pstart + counts - tile_start[:, None], 0, MOE_TILE), 0), axis=1)
    tile_e = jnp.minimum(jnp.sum((pend[None, :] <= tile_start[:, None]).astype(I32), axis=1), ne - 1)

    eids = jnp.arange(ne, dtype=I32)
    has_rows = counts > 0
    nxt_e = lax.cummin(jnp.where(has_rows, eids, ne), reverse=True)
    nxt_e = jnp.concatenate([nxt_e[1:], jnp.full((1,), ne, I32)])
    slot_e = (jnp.cumsum(has_rows.astype(I32)) - 1) % 2
    pick = lambda tab: jnp.sum(jnp.where(tile_e[:, None] == eids[None, :], tab[None, :], 0), axis=1).astype(I32)

    xs = _dispatch(dest, u2, rows_max, tok_tile)
    n_used = jnp.sum((tile_nv > 0).astype(I32))
    tile_blk = jnp.minimum(jnp.arange(n_tiles, dtype=I32), jnp.maximum(n_used - 1, 0))
    ys = _experts(tile_e, tile_nv.astype(I32), pick(nxt_e), pick(slot_e), tile_blk, xs,
                  w_exp_gate[0], w_exp_up[0], w_exp_down[0])

    wt = wsel.T
    y_p = _final(dest, pre_p, ada_p, wt, row2(ln2_w), row2(ln2_b), ys, 1, _tile_t(tp, FINAL_ROWS), 0)
    y_s = _final(dest, pre_s, ada_s, wt, row2(ln2_w), row2(ln2_b), ys, min(bs, max(1, FINAL_ROWS // ts)), ts, n_p)

    lead = lambda a: a[None]
    return (y_p, y_s) + tuple(lead(a) for a in st_p) + tuple(lead(a) for a in st_s)
```
